```python
import math
import jax, jax.numpy as jnp
from jax import lax
import numpy as np

D_MODEL = 2048
BATCH = 2
SEQ = 4096
DEPTH = 2

N_A_LAYERS = DEPTH // 2
N_B_LAYERS = DEPTH - N_A_LAYERS
HEAD_DIM = 128
N_HEADS = D_MODEL // HEAD_DIM
DILATED_GROUPS = ((128, 1), (512, 4), (2048, 16))
N_GROUPS = len(DILATED_GROUPS)
ATTN_BLOCK = 128
CONV_WIDTH = 31
D_FF = 4 * D_MODEL
PLE_DIM = 256
ROPE_THETA = 10000.0
LN_EPS = 1e-5
DEEPNORM_ALPHA = (2 * DEPTH) ** 0.25
DEEPNORM_BETA = (8 * DEPTH) ** -0.25

kernel_name = "yoco_conformer_dilated_hybrid"


def layer_norm(x, g, b):
    xf = x.astype(jnp.float32)
    mu = jnp.mean(xf, axis=-1, keepdims=True)
    var = jnp.mean(jnp.square(xf - mu), axis=-1, keepdims=True)
    y = (xf - mu) * lax.rsqrt(var + LN_EPS) * g.astype(jnp.float32) + b.astype(jnp.float32)
    return y.astype(x.dtype)


def rotary(x, positions):
    half = HEAD_DIM // 2
    inv_freq = ROPE_THETA ** (-jnp.arange(half, dtype=jnp.float32) * (2.0 / HEAD_DIM))
    ang = positions.astype(jnp.float32)[..., None] * inv_freq
    cos = jnp.cos(ang)[:, :, None, :]
    sin = jnp.sin(ang)[:, :, None, :]
    xf = x.astype(jnp.float32)
    x1, x2 = xf[..., :half], xf[..., half:]
    return jnp.concatenate([x1 * cos - x2 * sin, x2 * cos + x1 * sin], axis=-1).astype(x.dtype)


def conformer_conv(x, w_in, b_in, dw, dw_b, ln_g, ln_b, w_out):
    u = x @ w_in + b_in
    a, g = jnp.split(u, 2, axis=-1)
    u = a * jax.nn.sigmoid(g)
    u = lax.conv_general_dilated(
        u, dw.astype(u.dtype), window_strides=(1,), padding=[(CONV_WIDTH - 1, 0)],
        dimension_numbers=("NWC", "WIO", "NWC"), feature_group_count=D_MODEL) + dw_b
    u = jax.nn.silu(layer_norm(u, ln_g, ln_b))
    return u @ w_out


def to_strided_blocks(t, dil):
    b_, s_pad, h, hd = t.shape
    length = s_pad // dil
    t = t.reshape(b_, length, dil, h, hd).transpose(0, 2, 3, 1, 4)
    return t.reshape(b_, dil, h, length // ATTN_BLOCK, ATTN_BLOCK, hd)


def with_previous_block(t):
    prev = jnp.pad(t, ((0, 0), (0, 0), (0, 0), (1, 0), (0, 0), (0, 0)))[:, :, :, :-1]
    return jnp.concatenate([prev, t], axis=4)


def dilated_window_attention(q, k, v, window, dil):
    n_back = window // dil
    b_, s, h, hd = q.shape
    span = dil * ATTN_BLOCK
    s_pad = -(-s // span) * span
    pad = ((0, 0), (0, s_pad - s), (0, 0), (0, 0))
    qb = to_strided_blocks(jnp.pad(q, pad), dil)
    kk = with_previous_block(to_strided_blocks(jnp.pad(k, pad), dil))
    vv = with_previous_block(to_strided_blocks(jnp.pad(v, pad), dil))
    nb = qb.shape[3]
    sc = jnp.einsum("brhnqc,brhnkc->brhnqk", qb, kk,
                    preferred_element_type=jnp.float32) * (HEAD_DIM ** -0.5)
    qi = jnp.arange(ATTN_BLOCK)[:, None]
    kj = jnp.arange(2 * ATTN_BLOCK)[None, :]
    delta = qi + ATTN_BLOCK - kj
    key_sub = jnp.arange(nb)[:, None, None] * ATTN_BLOCK + kj - ATTN_BLOCK
    valid = (delta >= 0) & (delta <= n_back) & (key_sub >= 0)
    sc = jnp.where(valid, sc, -jnp.inf)
    m = jnp.max(sc, axis=-1, keepdims=True)
    pexp = jnp.exp(sc - m)
    l = jnp.sum(pexp, axis=-1, keepdims=True)
    o = jnp.einsum("brhnqk,brhnkc->brhnqc", pexp, vv.astype(jnp.float32)) / l
    lse = (m + jnp.log(l))[..., 0]
    o = o.transpose(0, 3, 4, 1, 2, 5).reshape(b_, s_pad, h, hd)[:, :s]
    lse = lse.transpose(0, 3, 4, 1, 2).reshape(b_, s_pad, h)[:, :s]
    return o, lse


def shared_kv(x, g, b, w_kv, positions):
    b_, s, _ = x.shape
    kv = (layer_norm(x, g, b) @ w_kv).reshape(b_, s, 2, N_HEADS, HEAD_DIM)
    return rotary(kv[:, :, 0], positions), kv[:, :, 1]


def dilated_mixer(x, w_q, k, v, w_o, positions):
    b_, s, _ = x.shape
    q = (x @ w_q).reshape(b_, s, N_GROUPS, N_HEADS, HEAD_DIM)
    outs, lses = [], []
    for g, (window, dil) in enumerate(DILATED_GROUPS):
        o, lse = dilated_window_attention(rotary(q[:, :, g], positions), k, v, window, dil)
        outs.append(o)
        lses.append(lse)
    wts = jax.nn.softmax(jnp.stack(lses), axis=0)
    o = jnp.sum(wts[..., None] * jnp.stack(outs), axis=0)
    return o.reshape(b_, s, D_MODEL).astype(x.dtype) @ w_o


def sq_relu_mlp(x, w_up, w_down):
    return jnp.square(jax.nn.relu(x @ w_up)) @ w_down


def per_layer_embedding(x, p_i, w_proj, w_gate):
    return (p_i @ w_proj) * jax.nn.sigmoid(x @ w_gate)


def setup_inputs(seed: int = 0) -> dict:
    key = jax.random.key(seed)
    ks = jax.random.split(key, 24)
    f32 = jnp.float32
    D = D_MODEL

    def w(k, shape, fan_in, scale=1.0):
        return jax.random.normal(k, shape, f32) * (fan_in ** -0.5) * scale

    def gain(k, shape):
        return 1.0 + 0.02 * jax.random.normal(k, shape, f32)

    def bias(k, shape):
        return 0.02 * jax.random.normal(k, shape, f32)

    x = jax.random.normal(ks[0], (BATCH, SEQ, D), f32)
    p = jax.random.normal(ks[1], (DEPTH, BATCH, SEQ, PLE_DIM), f32)
    offsets = jax.random.randint(ks[2], (BATCH, 1), 0, 1024, dtype=jnp.int32)
    positions = (jnp.arange(SEQ, dtype=jnp.int32)[None, :] + offsets).astype(jnp.int32)

    w_k = w(ks[10], (D, D), D)
    w_v = w(ks[11], (D, D), D, DEEPNORM_BETA)
    w_kv = jnp.concatenate([w_k, w_v], axis=-1)

    return {
        "x": x,
        "p": p,
        "positions": positions,
        "conv_w_in": w(ks[3], (N_A_LAYERS, D, 2 * D), D),
        "conv_b_in": bias(ks[4], (N_A_LAYERS, 2 * D)),
        "conv_dw": w(ks[5], (N_A_LAYERS, CONV_WIDTH, 1, D), CONV_WIDTH),
        "conv_dw_b": bias(ks[6], (N_A_LAYERS, D)),
        "conv_ln_g": gain(ks[7], (N_A_LAYERS, D)),
        "conv_ln_b": bias(ks[8], (N_A_LAYERS, D)),
        "conv_w_out": w(ks[9], (N_A_LAYERS, D, D), D, DEEPNORM_BETA),
        "kv_ln_g": gain(ks[12], (D,)),
        "kv_ln_b": bias(ks[13], (D,)),
        "w_kv": w_kv,
        "attn_w_q": w(ks[14], (N_B_LAYERS, D, N_GROUPS * D), D),
        "attn_w_o": w(ks[15], (N_B_LAYERS, D, D), D, DEEPNORM_BETA),
        "ln1_g": gain(ks[16], (DEPTH, D)),
        "ln1_b": bias(ks[17], (DEPTH, D)),
        "mlp_up": w(ks[18], (DEPTH, D, D_FF), D),
        "mlp_down": w(ks[19], (DEPTH, D_FF, D), D_FF, DEEPNORM_BETA),
        "ln2_g": gain(ks[20], (DEPTH, D)),
        "ln2_b": bias(ks[21], (DEPTH, D)),
        "ple_proj": w(ks[22], (DEPTH, PLE_DIM, D), PLE_DIM),
        "ple_gate": w(ks[23], (DEPTH, D, D), D),
    }


def reference(x, p, positions, conv_w_in, conv_b_in, conv_dw, conv_dw_b, conv_ln_g, conv_ln_b,
              conv_w_out, kv_ln_g, kv_ln_b, w_kv, attn_w_q, attn_w_o, ln1_g, ln1_b, mlp_up, mlp_down,
              ln2_g, ln2_b, ple_proj, ple_gate):
    k = v = None
    for i in range(DEPTH):
        if i < N_A_LAYERS:
            mix = conformer_conv(x, conv_w_in[i], conv_b_in[i], conv_dw[i], conv_dw_b[i],
                                 conv_ln_g[i], conv_ln_b[i], conv_w_out[i])
        else:
            j = i - N_A_LAYERS
            mix = dilated_mixer(x, attn_w_q[j], k, v, attn_w_o[j], positions)
        x = layer_norm(DEEPNORM_ALPHA * x + mix, ln1_g[i], ln1_b[i])
        x = layer_norm(DEEPNORM_ALPHA * x + sq_relu_mlp(x, mlp_up[i], mlp_down[i]), ln2_g[i], ln2_b[i])
        x = x + per_layer_embedding(x, p[i], ple_proj[i], ple_gate[i])
        if i == N_A_LAYERS - 1:
            k, v = shared_kv(x, kv_ln_g, kv_ln_b, w_kv, positions)
    return x
```

```python
import functools

import jax
import jax.numpy as jnp
from jax import lax
from jax.experimental import pallas as pl
from jax.experimental.pallas import tpu as pltpu

D_MODEL = 2048
DEPTH = 2
HEAD_DIM = 128
N_HEADS = D_MODEL // HEAD_DIM
DILATIONS = (1, 4, 16)
N_GROUPS = len(DILATIONS)
ATTN_BLOCK = 128
ATTN_TILE = ATTN_BLOCK * DILATIONS[-1]
CONV_WIDTH = 31
CONV_HALO = 32
D_FF = 4 * D_MODEL
PLE_DIM = 256
ROPE_THETA = 10000.0
LN_EPS = 1e-5
DEEPNORM_ALPHA = (2 * DEPTH) ** 0.25

VMEM_LIMIT_BYTES = 56 * 1024 * 1024
BF16 = jnp.bfloat16
F32 = jnp.float32


def _params(*semantics):
    return pltpu.CompilerParams(dimension_semantics=semantics, vmem_limit_bytes=VMEM_LIMIT_BYTES)


def _layer_norm(v, g, b):
    mu = jnp.mean(v, axis=-1, keepdims=True)
    c = v - mu
    var = jnp.mean(c * c, axis=-1, keepdims=True)
    return c * lax.rsqrt(var + LN_EPS) * g + b


def _mm(a, b):
    return jnp.dot(a, b, preferred_element_type=F32)


CONV_TM = 256
CONV_TN = 1024
CONV_ROWS = 16


def _glu_conv_kernel(x_ref, wa_ref, wg_ref, ba_ref, bg_ref, dw_ref, dwb_ref, o_ref, ubuf):
    i = pl.program_id(2)

    @pl.when(i == 0)
    def _():
        ubuf[pl.ds(0, CONV_HALO)] = jnp.zeros((CONV_HALO, 1, CONV_TN), F32)

    xb = x_ref[0].astype(BF16)
    a = _mm(xb, wa_ref[...]) + ba_ref[...]
    g = _mm(xb, wg_ref[...]) + bg_ref[...]
    ubuf[pl.ds(CONV_HALO, CONV_TM), 0, :] = a * jax.nn.sigmoid(g)

    taps = [dw_ref[k, 0] for k in range(CONV_WIDTH)]
    bias = dwb_ref[0, 0]
    first = CONV_HALO - (CONV_WIDTH - 1)

    def body(c, carry):
        t0 = c * CONV_ROWS
        acc = [bias] * CONV_ROWS
        for e in range(CONV_ROWS + CONV_WIDTH - 1):
            row = ubuf[t0 + first + e, 0]
            for r in range(max(0, e - CONV_WIDTH + 1), min(CONV_ROWS, e + 1)):
                acc[r] = acc[r] + taps[e - r] * row
        for r in range(CONV_ROWS):
            o_ref[0, t0 + r, 0] = acc[r]
        return carry

    lax.fori_loop(0, CONV_TM // CONV_ROWS, body, 0)
    ubuf[pl.ds(0, CONV_HALO)] = ubuf[pl.ds(CONV_TM, CONV_HALO)]


def _glu_conv(x, w_in, b_in, dw, dw_b):
    b_, s, d = x.shape
    nj = d // CONV_TN
    return pl.pallas_call(
        _glu_conv_kernel,
        grid=(nj, b_, s // CONV_TM),
        in_specs=[
            pl.BlockSpec((1, CONV_TM, d), lambda j, b, i: (b, i, 0)),
            pl.BlockSpec((d, CONV_TN), lambda j, b, i: (0, j)),
            pl.BlockSpec((d, CONV_TN), lambda j, b, i: (0, j + nj)),
            pl.BlockSpec((1, CONV_TN), lambda j, b, i: (0, j)),
            pl.BlockSpec((1, CONV_TN), lambda j, b, i: (0, j + nj)),
            pl.BlockSpec((CONV_WIDTH, 1, CONV_TN), lambda j, b, i: (0, 0, j)),
            pl.BlockSpec((1, 1, CONV_TN), lambda j, b, i: (0, 0, j)),
        ],
        out_specs=pl.BlockSpec((1, CONV_TM, 1, CONV_TN), lambda j, b, i: (b, i, 0, j)),
        out_shape=jax.ShapeDtypeStruct((b_, s, 1, d), F32),
        scratch_shapes=[pltpu.VMEM((CONV_TM + CONV_HALO, 1, CONV_TN), F32)],
        compiler_params=_params("arbitrary", "arbitrary", "arbitrary"),
        name="glu_conv",
    )(x, w_in, w_in, b_in, b_in, dw, dw_b)


ROW_TM = 256


def _proj_ln_kernel(act_ref, x_ref, w_ref, pg_ref, pb_ref, g_ref, b_ref, o_ref, *, conv_prologue):
    a = act_ref[...]
    if conv_prologue:
        a = _layer_norm(a, pg_ref[...], pb_ref[...])
        a = a * jax.nn.sigmoid(a)
    mix = _mm(a.astype(BF16), w_ref[...])
    o_ref[...] = _layer_norm(DEEPNORM_ALPHA * x_ref[...] + mix, g_ref[...], b_ref[...])


def _proj_ln(act, x, w, pro_g, pro_b, g, b, *, conv_prologue, name):
    t, d = x.shape
    row = pl.BlockSpec((ROW_TM, d), lambda i: (i, 0))
    vec = pl.BlockSpec((1, d), lambda i: (0, 0))
    return pl.pallas_call(
        functools.partial(_proj_ln_kernel, conv_prologue=conv_prologue),
        grid=(t // ROW_TM,),
        in_specs=[row, row, pl.BlockSpec((d, d), lambda i: (0, 0)), vec, vec, vec, vec],
        out_specs=row,
        out_shape=jax.ShapeDtypeStruct((t, d), F32),
        compiler_params=_params("arbitrary"),
        name=name,
    )(act, x, w, pro_g, pro_b, g, b)


MLP_TM = 512
MLP_TF = 512


def _mlp_kernel(x_ref, up_ref, down_ref, g_ref, b_ref, o_ref, xb_ref, acc_ref):
    k = pl.program_id(1)

    @pl.when(k == 0)
    def _():
        xb_ref[...] = x_ref[...].astype(BF16)

    h = jnp.maximum(_mm(xb_ref[...], up_ref[...]), 0.0)
    part = _mm((h * h).astype(BF16), down_ref[...])

    @pl.when(k == 0)
    def _():
        acc_ref[...] = part

    @pl.when(k > 0)
    def _():
        acc_ref[...] += part

    @pl.when(k == pl.num_programs(1) - 1)
    def _():
        o_ref[...] = _layer_norm(DEEPNORM_ALPHA * x_ref[...] + acc_ref[...], g_ref[...], b_ref[...])


def _mlp(x, up, down, g, b, *, name):
    t, d = x.shape
    ff = up.shape[1]
    row = pl.BlockSpec((MLP_TM, d), lambda i, k: (i, 0))
    vec = pl.BlockSpec((1, d), lambda i, k: (0, 0))
    return pl.pallas_call(
        _mlp_kernel,
        grid=(t // MLP_TM, ff // MLP_TF),
        in_specs=[row, pl.BlockSpec((d, MLP_TF), lambda i, k: (0, k)),
                  pl.BlockSpec((MLP_TF, d), lambda i, k: (k, 0)), vec, vec],
        out_specs=row,
        out_shape=jax.ShapeDtypeStruct((t, d), F32),
        scratch_shapes=[pltpu.VMEM((MLP_TM, d), BF16), pltpu.VMEM((MLP_TM, d), F32)],
        compiler_params=_params("arbitrary", "arbitrary"),
        name=name,
    )(x, up, down, g, b)


def _ple_kernel(x_ref, p_ref, proj_ref, gate_ref, o_ref):
    x = x_ref[...]
    emb = _mm(p_ref[...].astype(BF16), proj_ref[...])
    gate = jax.nn.sigmoid(_mm(x.astype(BF16), gate_ref[...]))
    o_ref[...] = x + emb * gate


def _ple(x, p, proj, gate, *, name):
    t, d = x.shape
    row = pl.BlockSpec((ROW_TM, d), lambda i: (i, 0))
    return pl.pallas_call(
        _ple_kernel,
        grid=(t // ROW_TM,),
        in_specs=[row, pl.BlockSpec((ROW_TM, PLE_DIM), lambda i: (i, 0)),
                  pl.BlockSpec((PLE_DIM, d), lambda i: (0, 0)), pl.BlockSpec((d, d), lambda i: (0, 0))],
        out_specs=row,
        out_shape=jax.ShapeDtypeStruct((t, d), F32),
        compiler_params=_params("arbitrary"),
        name=name,
    )(x, p, proj, gate)


ROPE_TM = 512


def _rope_table_kernel(pos_ref, freq_ref, cos_ref, sin_ref):
    ang = pos_ref[...] * freq_ref[...]
    lane = lax.broadcasted_iota(jnp.int32, ang.shape, 1)
    cos_ref[...] = jnp.cos(ang)
    sin_ref[...] = jnp.where(lane < HEAD_DIM // 2, -1.0, 1.0) * jnp.sin(ang)


def _rope_tables(positions):
    t = positions.size
    half = HEAD_DIM // 2
    inv_freq = ROPE_THETA ** (-jnp.arange(half, dtype=F32) * (2.0 / HEAD_DIM))
    freq = jnp.concatenate([inv_freq, inv_freq]).reshape(1, HEAD_DIM)
    pos = positions.astype(F32).reshape(t, 1)
    tab = pl.BlockSpec((ROPE_TM, HEAD_DIM), lambda i: (i, 0))
    return pl.pallas_call(
        _rope_table_kernel,
        grid=(t // ROPE_TM,),
        in_specs=[pl.BlockSpec((ROPE_TM, 1), lambda i: (i, 0)), pl.BlockSpec((1, HEAD_DIM), lambda i: (0, 0))],
        out_specs=[tab, tab],
        out_shape=[jax.ShapeDtypeStruct((t, HEAD_DIM), F32)] * 2,
        compiler_params=_params("arbitrary"),
        name="rope_tables",
    )(pos, freq)


def _rotary(v, cos, sin_signed):
    return v * cos + pltpu.roll(v, HEAD_DIM // 2, axis=1) * sin_signed


HEADS_TM = 512
HEADS_TN = 512


def _heads_kernel(x_ref, w_ref, cos_ref, sin_ref, g_ref, b_ref, o_ref, xb_ref, *, ln_prologue, n_rotary_tiles):
    j = pl.program_id(2)

    @pl.when(j == 0)
    def _():
        xv = x_ref[0]
        if ln_prologue:
            xv = _layer_norm(xv, g_ref[...], b_ref[...])
        xb_ref[...] = xv.astype(BF16)

    res = _mm(xb_ref[...], w_ref[...])

    def store(rotate):
        for hh in range(HEADS_TN // HEAD_DIM):
            v = res[:, hh * HEAD_DIM:(hh + 1) * HEAD_DIM]
            o_ref[0, hh] = _rotary(v, cos_ref[...], sin_ref[...]) if rotate else v

    if n_rotary_tiles is None:
        store(True)
    else:
        pl.when(j < n_rotary_tiles)(functools.partial(store, True))
        pl.when(j >= n_rotary_tiles)(functools.partial(store, False))


def _heads_proj(x, w, cos, sin, g, b, *, ln_prologue, n_rotary_tiles, name):
    b_, s, d = x.shape
    n = w.shape[1]
    per_b = s // HEADS_TM
    tab = pl.BlockSpec((HEADS_TM, HEAD_DIM), lambda bb, i, j: (bb * per_b + i, 0))
    vec = pl.BlockSpec((1, d), lambda bb, i, j: (0, 0))
    return pl.pallas_call(
        functools.partial(_heads_kernel, ln_prologue=ln_prologue, n_rotary_tiles=n_rotary_tiles),
        grid=(b_, per_b, n // HEADS_TN),
        in_specs=[pl.BlockSpec((1, HEADS_TM, d), lambda bb, i, j: (bb, i, 0)),
                  pl.BlockSpec((d, HEADS_TN), lambda bb, i, j: (0, j)), tab, tab, vec, vec],
        out_specs=pl.BlockSpec((1, HEADS_TN // HEAD_DIM, HEADS_TM, HEAD_DIM), lambda bb, i, j: (bb, j, i, 0)),
        out_shape=jax.ShapeDtypeStruct((b_, n // HEAD_DIM, s, HEAD_DIM), F32),
        scratch_shapes=[pltpu.VMEM((HEADS_TM, d), BF16)],
        compiler_params=_params("arbitrary", "arbitrary", "arbitrary"),
        name=name,
    )(x, w, cos, sin, g, b)


def _attn_kernel(q0_ref, q1_ref, q2_ref, kc_ref, kp_ref, vc_ref, vp_ref, o_ref, acc_ref, m_ref, l_ref):
    tile = pl.program_id(2)
    scale = HEAD_DIM ** -0.5
    qi = lax.broadcasted_iota(jnp.int32, (ATTN_BLOCK, ATTN_BLOCK), 0)
    kj = lax.broadcasted_iota(jnp.int32, (ATTN_BLOCK, ATTN_BLOCK), 1)
    cur_ok = kj <= qi
    prev_ok = kj >= qi
    neg = -jnp.inf
    nt_dims = (((1,), (1,)), ((), ()))

    for g, (q_ref, dil) in enumerate(zip((q0_ref, q1_ref, q2_ref), DILATIONS)):
        n_blocks = ATTN_TILE // (ATTN_BLOCK * dil)
        for r in range(dil):
            for n in range(n_blocks):
                rows = pl.ds(n * ATTN_BLOCK * dil + r, ATTN_BLOCK, stride=dil)
                q = q_ref[0, 0, rows, :].astype(BF16)
                k_cur = kc_ref[0, 0, rows, :].astype(BF16)
                v_cur = vc_ref[0, 0, rows, :].astype(BF16)
                if n > 0:
                    prow = pl.ds((n - 1) * ATTN_BLOCK * dil + r, ATTN_BLOCK, stride=dil)
                    k_prev = kc_ref[0, 0, prow, :].astype(BF16)
                    v_prev = vc_ref[0, 0, prow, :].astype(BF16)
                    ok_prev = prev_ok
                else:
                    prow = pl.ds(ATTN_TILE - ATTN_BLOCK * dil + r, ATTN_BLOCK, stride=dil)
                    k_prev = kp_ref[0, 0, prow, :].astype(BF16)
                    v_prev = vp_ref[0, 0, prow, :].astype(BF16)
                    ok_prev = jnp.logical_and(prev_ok, tile > 0)
                s_cur = jnp.where(cur_ok, lax.dot_general(q, k_cur, nt_dims, preferred_element_type=F32) * scale, neg)
                s_prev = jnp.where(ok_prev, lax.dot_general(q, k_prev, nt_dims, preferred_element_type=F32) * scale, neg)
                m_blk = jnp.maximum(jnp.max(s_cur, axis=1, keepdims=True), jnp.max(s_prev, axis=1, keepdims=True))
                if g == 0:
                    m_new = m_blk
                else:
                    m_old = m_ref[rows, :]
                    m_new = jnp.maximum(m_old, m_blk)
                p_cur = jnp.exp(s_cur - m_new)
                p_prev = jnp.exp(s_prev - m_new)
                l_blk = jnp.sum(p_cur, axis=1, keepdims=True) + jnp.sum(p_prev, axis=1, keepdims=True)
                pv = _mm(p_cur.astype(BF16), v_cur) + _mm(p_prev.astype(BF16), v_prev)
                if g == 0:
                    acc_ref[rows, :] = pv
                    l_ref[rows, :] = jnp.broadcast_to(l_blk, (ATTN_BLOCK, HEAD_DIM))
                    m_ref[rows, :] = jnp.broadcast_to(m_new, (ATTN_BLOCK, HEAD_DIM))
                else:
                    alpha = jnp.exp(m_old - m_new)
                    acc_ref[rows, :] = acc_ref[rows, :] * alpha + pv
                    l_ref[rows, :] = l_ref[rows, :] * alpha + l_blk
                    m_ref[rows, :] = m_new
    o_ref[0] = acc_ref[...] / l_ref[...]


def _attention(q, kv):
    b_, _, s, hd = q.shape
    n_tiles = s // ATTN_TILE

    def q_spec(g):
        return pl.BlockSpec((1, 1, ATTN_TILE, hd), lambda b, h, t: (b, g * N_HEADS + h, t, 0))

    def kv_spec(head_offset, prev):
        if prev:
            return pl.BlockSpec((1, 1, ATTN_TILE, hd), lambda b, h, t: (b, head_offset + h, jnp.maximum(t - 1, 0), 0))
        return pl.BlockSpec((1, 1, ATTN_TILE, hd), lambda b, h, t: (b, head_offset + h, t, 0))

    return pl.pallas_call(
        _attn_kernel,
        grid=(b_, N_HEADS, n_tiles),
        in_specs=[q_spec(0), q_spec(1), q_spec(2),
                  kv_spec(0, False), kv_spec(0, True), kv_spec(N_HEADS, False), kv_spec(N_HEADS, True)],
        out_specs=pl.BlockSpec((1, ATTN_TILE, hd), lambda b, h, t: (b, t, h)),
        out_shape=jax.ShapeDtypeStruct((b_, s, N_HEADS * hd), F32),
        scratch_shapes=[pltpu.VMEM((ATTN_TILE, hd), F32)] * 3,
        compiler_params=_params("arbitrary", "arbitrary", "arbitrary"),
        name="dilated_attention",
    )(q, q, q, kv, kv, kv, kv)


def kernel(x, p, positions, conv_w_in, conv_b_in, conv_dw, conv_dw_b, conv_ln_g, conv_ln_b, conv_w_out,
           kv_ln_g, kv_ln_b, w_kv, attn_w_q, attn_w_o, ln1_g, ln1_b, mlp_up, mlp_down, ln2_g, ln2_b,
           ple_proj, ple_gate):
    b_, s, d = x.shape
    t = b_ * s
    assert s % ATTN_TILE == 0 and d == D_MODEL

    def vec(v):
        return v.reshape(1, -1)

    def wb(w):
        return w.astype(BF16)

    cos, sin = _rope_tables(positions)

    conv = _glu_conv(x, wb(conv_w_in[0]), vec(conv_b_in[0]), conv_dw[0], conv_dw_b[0].reshape(1, 1, d))
    xf = x.reshape(t, d)
    xf = _proj_ln(conv.reshape(t, d), xf, wb(conv_w_out[0]), vec(conv_ln_g[0]), vec(conv_ln_b[0]),
                  vec(ln1_g[0]), vec(ln1_b[0]), conv_prologue=True, name="conv_out_ln1")
    xf = _mlp(xf, wb(mlp_up[0]), wb(mlp_down[0]), vec(ln2_g[0]), vec(ln2_b[0]), name="mlp0")
    xf = _ple(xf, p[0].reshape(t, PLE_DIM), wb(ple_proj[0]), wb(ple_gate[0]), name="ple0")

    x3 = xf.reshape(b_, s, d)
    kv = _heads_proj(x3, wb(w_kv), cos, sin, vec(kv_ln_g), vec(kv_ln_b), ln_prologue=True,
                     n_rotary_tiles=d // HEADS_TN, name="kv_proj")
    q = _heads_proj(x3, wb(attn_w_q[0]), cos, sin, vec(kv_ln_g), vec(kv_ln_b), ln_prologue=False,
                    n_rotary_tiles=None, name="q_proj")
    o = _attention(q, kv)
    xf = _proj_ln(o.reshape(t, d), xf, wb(attn_w_o[0]), vec(ln1_g[1]), vec(ln1_b[1]), vec(ln1_g[1]), vec(ln1_b[1]),
                  conv_prologue=False, name="attn_out_ln1")
    xf = _mlp(xf, wb(mlp_up[1]), wb(mlp_down[1]), vec(ln2_g[1]), vec(ln2_b[1]), name="mlp1")
    xf = _ple(xf, p[1].reshape(t, PLE_DIM), wb(ple_proj[1]), wb(ple_gate[1]), name="ple1")
    return xf.reshape(b_, s, d)
```

```python
import functools

import jax
import jax.numpy as jnp
from jax import lax
from jax.experimental import pallas as pl
from jax.experimental.pallas import tpu as pltpu

D_MODEL = 2048
DEPTH = 2
HEAD_DIM = 128
N_HEADS = D_MODEL // HEAD_DIM
DILATIONS = (1, 4, 16)
N_GROUPS = len(DILATIONS)
ATTN_BLOCK = 128
ATTN_TILE = ATTN_BLOCK * DILATIONS[-1]
CONV_WIDTH = 31
CONV_HALO = 32
D_FF = 4 * D_MODEL
PLE_DIM = 256
ROPE_THETA = 10000.0
LN_EPS = 1e-5
DEEPNORM_ALPHA = (2 * DEPTH) ** 0.25

VMEM_LIMIT_BYTES = 56 * 1024 * 1024
BF16 = jnp.bfloat16
F32 = jnp.float32


def _params(*semantics):
    return pltpu.CompilerParams(dimension_semantics=semantics, vmem_limit_bytes=VMEM_LIMIT_BYTES)


def _layer_norm(v, g, b):
    mu = jnp.mean(v, axis=-1, keepdims=True)
    c = v - mu
    var = jnp.mean(c * c, axis=-1, keepdims=True)
    return c * lax.rsqrt(var + LN_EPS) * g + b


def _mm(a, b):
    return jnp.dot(a, b, preferred_element_type=F32)


CONV_TM = 256
CONV_TN = 1024
CONV_ROWS = 16


def _glu_conv_kernel(x_ref, wa_ref, wg_ref, ba_ref, bg_ref, dw_ref, dwb_ref, o_ref, ubuf):
    i = pl.program_id(2)

    @pl.when(i == 0)
    def _():
        ubuf[pl.ds(0, CONV_HALO)] = jnp.zeros((CONV_HALO, 1, CONV_TN), F32)

    xb = x_ref[0].astype(BF16)
    a = _mm(xb, wa_ref[...]) + ba_ref[...]
    g = _mm(xb, wg_ref[...]) + bg_ref[...]
    ubuf[pl.ds(CONV_HALO, CONV_TM), 0, :] = a * jax.nn.sigmoid(g)

    taps = [dw_ref[k, 0] for k in range(CONV_WIDTH)]
    bias = dwb_ref[0, 0]
    first = CONV_HALO - (CONV_WIDTH - 1)

    def body(c, carry):
        t0 = c * CONV_ROWS
        acc = [bias] * CONV_ROWS
        for e in range(CONV_ROWS + CONV_WIDTH - 1):
            row = ubuf[t0 + first + e, 0]
            for r in range(max(0, e - CONV_WIDTH + 1), min(CONV_ROWS, e + 1)):
                acc[r] = acc[r] + taps[e - r] * row
        for r in range(CONV_ROWS):
            o_ref[0, t0 + r, 0] = acc[r]
        return carry

    lax.fori_loop(0, CONV_TM // CONV_ROWS, body, 0)
    ubuf[pl.ds(0, CONV_HALO)] = ubuf[pl.ds(CONV_TM, CONV_HALO)]


def _glu_conv(x, w_in, b_in, dw, dw_b):
    b_, s, d = x.shape
    nj = d // CONV_TN
    return pl.pallas_call(
        _glu_conv_kernel,
        grid=(nj, b_, s // CONV_TM),
        in_specs=[
            pl.BlockSpec((1, CONV_TM, d), lambda j, b, i: (b, i, 0)),
            pl.BlockSpec((d, CONV_TN), lambda j, b, i: (0, j)),
            pl.BlockSpec((d, CONV_TN), lambda j, b, i: (0, j + nj)),
            pl.BlockSpec((1, CONV_TN), lambda j, b, i: (0, j)),
            pl.BlockSpec((1, CONV_TN), lambda j, b, i: (0, j + nj)),
            pl.BlockSpec((CONV_WIDTH, 1, CONV_TN), lambda j, b, i: (0, 0, j)),
            pl.BlockSpec((1, 1, CONV_TN), lambda j, b, i: (0, 0, j)),
        ],
        out_specs=pl.BlockSpec((1, CONV_TM, 1, CONV_TN), lambda j, b, i: (b, i, 0, j)),
        out_shape=jax.ShapeDtypeStruct((b_, s, 1, d), F32),
        scratch_shapes=[pltpu.VMEM((CONV_HALO + CONV_TM, 1, CONV_TN), F32)],
        compiler_params=_params("arbitrary", "arbitrary", "arbitrary"),
        name="glu_conv",
    )(x, w_in, w_in, b_in, b_in, dw, dw_b)


ROW_TM = 512


def _proj_ln_kernel(act_ref, x_ref, w_ref, pg_ref, pb_ref, g_ref, b_ref, o_ref, *, conv_prologue):
    a = act_ref[...]
    if conv_prologue:
        a = _layer_norm(a, pg_ref[...], pb_ref[...])
        a = a * jax.nn.sigmoid(a)
    mix = _mm(a.astype(BF16), w_ref[...])
    o_ref[...] = _layer_norm(DEEPNORM_ALPHA * x_ref[...] + mix, g_ref[...], b_ref[...])


def _proj_ln(act, x, w, pro_g, pro_b, g, b, *, conv_prologue, name):
    t, d = x.shape
    row = pl.BlockSpec((ROW_TM, d), lambda i: (i, 0))
    vec = pl.BlockSpec((1, d), lambda i: (0, 0))
    return pl.pallas_call(
        functools.partial(_proj_ln_kernel, conv_prologue=conv_prologue),
        grid=(t // ROW_TM,),
        in_specs=[row, row, pl.BlockSpec((d, d), lambda i: (0, 0)), vec, vec, vec, vec],
        out_specs=row,
        out_shape=jax.ShapeDtypeStruct((t, d), F32),
        compiler_params=_params("arbitrary"),
        name=name,
    )(act, x, w, pro_g, pro_b, g, b)


MLP_TM = 1024
MLP_TF = 512


def _mlp_kernel(x_ref, up_ref, down_ref, g_ref, b_ref, o_ref, xb_ref):
    k = pl.program_id(1)

    @pl.when(k == 0)
    def _():
        xb_ref[...] = x_ref[...].astype(BF16)
        o_ref[...] = jnp.zeros_like(o_ref)

    h = jnp.maximum(_mm(xb_ref[...], up_ref[...]), 0.0)
    o_ref[...] += _mm((h * h).astype(BF16), down_ref[...])

    @pl.when(k == pl.num_programs(1) - 1)
    def _():
        o_ref[...] = _layer_norm(DEEPNORM_ALPHA * x_ref[...] + o_ref[...], g_ref[...], b_ref[...])


def _mlp(x, up, down, g, b, *, layer, name):
    t, d = x.shape
    ff = up.shape[2]
    row = pl.BlockSpec((MLP_TM, d), lambda i, k: (i, 0))
    vec = pl.BlockSpec((1, d), lambda i, k: (0, 0))
    return pl.pallas_call(
        _mlp_kernel,
        grid=(t // MLP_TM, ff // MLP_TF),
        in_specs=[row, pl.BlockSpec((None, d, MLP_TF), lambda i, k: (layer, 0, k)),
                  pl.BlockSpec((None, MLP_TF, d), lambda i, k: (layer, k, 0)), vec, vec],
        out_specs=row,
        out_shape=jax.ShapeDtypeStruct((t, d), F32),
        scratch_shapes=[pltpu.VMEM((MLP_TM, d), BF16)],
        compiler_params=_params("arbitrary", "arbitrary"),
        name=name,
    )(x, up, down, g, b)


def _ple_kernel(x_ref, p_ref, proj_ref, gate_ref, o_ref):
    x = x_ref[...]
    emb = _mm(p_ref[...].astype(BF16), proj_ref[...])
    gate = jax.nn.sigmoid(_mm(x.astype(BF16), gate_ref[...]))
    o_ref[...] = x + emb * gate


def _ple(x, p, proj, gate, *, layer, name):
    t, d = x.shape
    row = pl.BlockSpec((ROW_TM, d), lambda i: (i, 0))
    return pl.pallas_call(
        _ple_kernel,
        grid=(t // ROW_TM,),
        in_specs=[row, pl.BlockSpec((None, ROW_TM, PLE_DIM), lambda i: (layer, i, 0)),
                  pl.BlockSpec((None, PLE_DIM, d), lambda i: (layer, 0, 0)),
                  pl.BlockSpec((None, d, d), lambda i: (layer, 0, 0))],
        out_specs=row,
        out_shape=jax.ShapeDtypeStruct((t, d), F32),
        compiler_params=_params("arbitrary"),
        name=name,
    )(x, p, proj, gate)


ROPE_TM = 512


def _rope_table_kernel(pos_ref, freq_ref, cos_ref, sin_ref):
    ang = pos_ref[...] * freq_ref[...]
    lane = lax.broadcasted_iota(jnp.int32, ang.shape, 1)
    cos_ref[...] = jnp.cos(ang)
    sin_ref[...] = jnp.where(lane < HEAD_DIM // 2, -1.0, 1.0) * jnp.sin(ang)


def _rope_tables(positions):
    t = positions.size
    half = HEAD_DIM // 2
    inv_freq = ROPE_THETA ** (-jnp.arange(half, dtype=F32) * (2.0 / HEAD_DIM))
    freq = jnp.concatenate([inv_freq, inv_freq]).reshape(1, HEAD_DIM)
    pos = positions.astype(F32).reshape(t, 1)
    tab = pl.BlockSpec((ROPE_TM, HEAD_DIM), lambda i: (i, 0))
    return pl.pallas_call(
        _rope_table_kernel,
        grid=(t // ROPE_TM,),
        in_specs=[pl.BlockSpec((ROPE_TM, 1), lambda i: (i, 0)), pl.BlockSpec((1, HEAD_DIM), lambda i: (0, 0))],
        out_specs=[tab, tab],
        out_shape=[jax.ShapeDtypeStruct((t, HEAD_DIM), F32)] * 2,
        compiler_params=_params("arbitrary"),
        name="rope_tables",
    )(pos, freq)


def _rotary(v, cos, sin_signed):
    return v * cos + pltpu.roll(v, HEAD_DIM // 2, axis=1) * sin_signed


HEADS_TM = 1024
HEADS_TN = 512


def _heads_kernel(x_ref, w_ref, cos_ref, sin_ref, g_ref, b_ref, o_ref, xb_ref, *, ln_prologue, n_rotary_tiles):
    j = pl.program_id(2)

    @pl.when(j == 0)
    def _():
        xv = x_ref[0]
        if ln_prologue:
            xv = _layer_norm(xv, g_ref[...], b_ref[...])
        xb_ref[...] = xv.astype(BF16)

    res = _mm(xb_ref[...], w_ref[...])

    def store(rotate):
        for hh in range(HEADS_TN // HEAD_DIM):
            v = res[:, hh * HEAD_DIM:(hh + 1) * HEAD_DIM]
            if rotate:
                v = _rotary(v, cos_ref[...], sin_ref[...])
            o_ref[0, hh] = v.astype(BF16)

    if n_rotary_tiles is None:
        store(True)
    else:
        pl.when(j < n_rotary_tiles)(functools.partial(store, True))
        pl.when(j >= n_rotary_tiles)(functools.partial(store, False))


def _heads_proj(x, w, cos, sin, g, b, *, ln_prologue, n_rotary_tiles, name):
    b_, s, d = x.shape
    n = w.shape[1]
    per_b = s // HEADS_TM
    tab = pl.BlockSpec((HEADS_TM, HEAD_DIM), lambda bb, i, j: (bb * per_b + i, 0))
    vec = pl.BlockSpec((1, d), lambda bb, i, j: (0, 0))
    return pl.pallas_call(
        functools.partial(_heads_kernel, ln_prologue=ln_prologue, n_rotary_tiles=n_rotary_tiles),
        grid=(b_, per_b, n // HEADS_TN),
        in_specs=[pl.BlockSpec((1, HEADS_TM, d), lambda bb, i, j: (bb, i, 0)),
                  pl.BlockSpec((d, HEADS_TN), lambda bb, i, j: (0, j)), tab, tab, vec, vec],
        out_specs=pl.BlockSpec((1, HEADS_TN // HEAD_DIM, HEADS_TM, HEAD_DIM), lambda bb, i, j: (bb, j, i, 0)),
        out_shape=jax.ShapeDtypeStruct((b_, n // HEAD_DIM, s, HEAD_DIM), BF16),
        scratch_shapes=[pltpu.VMEM((HEADS_TM, d), BF16)],
        compiler_params=_params("arbitrary", "arbitrary", "arbitrary"),
        name=name,
    )(x, w, cos, sin, g, b)


MERGE_ROWS = 256


def _attn_kernel(*refs):
    group_refs = [refs[5 * g:5 * g + 5] for g in range(N_GROUPS)]
    o_ref, og_ref, lse_ref, bias_ref = refs[5 * N_GROUPS:]
    tile = pl.program_id(2)
    scale = HEAD_DIM ** -0.5
    nt_dims = (((1,), (1,)), ((), ()))

    qi = lax.broadcasted_iota(jnp.int32, (ATTN_BLOCK, 2 * ATTN_BLOCK), 0)
    kj = lax.broadcasted_iota(jnp.int32, (ATTN_BLOCK, 2 * ATTN_BLOCK), 1)
    bias = jnp.where(jnp.logical_and(kj >= qi, kj <= qi + ATTN_BLOCK), 0.0, -jnp.inf)
    bias_ref[0] = bias
    bias_ref[1] = jnp.where(jnp.logical_or(tile > 0, kj >= ATTN_BLOCK), bias, -jnp.inf)

    for g, (dil, (q_ref, kc_ref, kp_ref, vc_ref, vp_ref)) in enumerate(zip(DILATIONS, group_refs)):
        n_blocks = ATTN_TILE // (ATTN_BLOCK * dil)
        for r in range(dil):
            lanes = slice(r * HEAD_DIM, (r + 1) * HEAD_DIM)
            k_prev = kp_ref[0, 0, :, lanes]
            v_prev = vp_ref[0, 0, :, lanes]
            for n in range(n_blocks):
                rows = slice(n * ATTN_BLOCK, (n + 1) * ATTN_BLOCK)
                q = q_ref[0, 0, rows, lanes]
                k_cur = kc_ref[0, 0, rows, lanes]
                v_cur = vc_ref[0, 0, rows, lanes]
                keys = jnp.concatenate([k_prev, k_cur], axis=0)
                vals = jnp.concatenate([v_prev, v_cur], axis=0)
                s = lax.dot_general(q, keys, nt_dims, preferred_element_type=F32) * scale
                s = s + bias_ref[1 if n == 0 else 0]
                m = jnp.max(s, axis=1, keepdims=True)
                p = jnp.exp(s - m)
                l = jnp.sum(p, axis=1, keepdims=True)
                pv = _mm(p.astype(BF16), vals)
                if dil == 1:
                    dst = pl.ds(n * ATTN_BLOCK, ATTN_BLOCK)
                else:
                    dst = pl.ds(n * ATTN_BLOCK * dil + r, ATTN_BLOCK, stride=dil)
                og_ref[g, dst, :] = pv / l
                lse_ref[g, dst, :] = jnp.broadcast_to(m + jnp.log(l), (ATTN_BLOCK, HEAD_DIM))
                k_prev, v_prev = k_cur, v_cur

    for c in range(ATTN_TILE // MERGE_ROWS):
        rows = slice(c * MERGE_ROWS, (c + 1) * MERGE_ROWS)
        lses = [lse_ref[g, rows, :] for g in range(N_GROUPS)]
        top = functools.reduce(jnp.maximum, lses)
        wts = [jnp.exp(v - top) for v in lses]
        num = sum(w * og_ref[g, rows, :] for g, w in enumerate(wts))
        o_ref[0, rows, :] = (num / sum(wts)).astype(o_ref.dtype)


def _attention(q, kv):
    b_, n_q, s, hd = q.shape
    n_kv = kv.shape[1]
    n_tiles = s // ATTN_TILE
    operands, in_specs = [], []
    for g, dil in enumerate(DILATIONS):
        rows, width = ATTN_TILE // dil, dil * hd
        prev_per_tile = rows // ATTN_BLOCK
        qv = q.reshape(b_, n_q, s // dil, width)
        kvv = kv.reshape(b_, n_kv, s // dil, width)

        def cur(head_offset):
            return pl.BlockSpec((1, 1, rows, width), lambda b, h, t, o=head_offset: (b, o + h, t, 0))

        def prev(head_offset):
            return pl.BlockSpec((1, 1, ATTN_BLOCK, width),
                                lambda b, h, t, o=head_offset, n=prev_per_tile: (b, o + h, jnp.maximum(n * t - 1, 0), 0))

        operands += [qv, kvv, kvv, kvv, kvv]
        in_specs += [cur(g * N_HEADS), cur(0), prev(0), cur(N_HEADS), prev(N_HEADS)]

    return pl.pallas_call(
        _attn_kernel,
        grid=(b_, N_HEADS, n_tiles),
        in_specs=in_specs,
        out_specs=pl.BlockSpec((1, ATTN_TILE, hd), lambda b, h, t: (b, t, h)),
        out_shape=jax.ShapeDtypeStruct((b_, s, N_HEADS * hd), BF16),
        scratch_shapes=[pltpu.VMEM((N_GROUPS, ATTN_TILE, hd), F32), pltpu.VMEM((N_GROUPS, ATTN_TILE, hd), F32),
                        pltpu.VMEM((2, ATTN_BLOCK, 2 * ATTN_BLOCK), F32)],
        compiler_params=_params("arbitrary", "arbitrary", "arbitrary"),
        name="dilated_attention",
    )(*operands)


def kernel(x, p, positions, conv_w_in, conv_b_in, conv_dw, conv_dw_b, conv_ln_g, conv_ln_b, conv_w_out,
           kv_ln_g, kv_ln_b, w_kv, attn_w_q, attn_w_o, ln1_g, ln1_b, mlp_up, mlp_down, ln2_g, ln2_b,
           ple_proj, ple_gate):
    b_, s, d = x.shape
    t = b_ * s
    assert s % ATTN_TILE == 0 and d == D_MODEL

    def vec(v):
        return v.reshape(1, -1)

    def wb(w):
        return w.astype(BF16)

    cos, sin = _rope_tables(positions)

    up, down, proj, gate = wb(mlp_up), wb(mlp_down), wb(ple_proj), wb(ple_gate)
    pf = p.reshape(DEPTH, t, PLE_DIM)
    conv = _glu_conv(x, wb(conv_w_in[0]), vec(conv_b_in[0]), conv_dw[0], conv_dw_b[0].reshape(1, 1, d))
    xf = x.reshape(t, d)
    xf = _proj_ln(conv.reshape(t, d), xf, wb(conv_w_out[0]), vec(conv_ln_g[0]), vec(conv_ln_b[0]),
                  vec(ln1_g[0]), vec(ln1_b[0]), conv_prologue=True, name="conv_out_ln1")
    xf = _mlp(xf, up, down, vec(ln2_g[0]), vec(ln2_b[0]), layer=0, name="mlp0")
    xf = _ple(xf, pf, proj, gate, layer=0, name="ple0")

    x3 = xf.reshape(b_, s, d)
    kv = _heads_proj(x3, wb(w_kv), cos, sin, vec(kv_ln_g), vec(kv_ln_b), ln_prologue=True,
                     n_rotary_tiles=d // HEADS_TN, name="kv_proj")
    q = _heads_proj(x3, wb(attn_w_q[0]), cos, sin, vec(kv_ln_g), vec(kv_ln_b), ln_prologue=False,
                    n_rotary_tiles=None, name="q_proj")
    o = _attention(q, kv)
    xf = _proj_ln(o.reshape(t, d), xf, wb(attn_w_o[0]), vec(ln1_g[1]), vec(ln1_b[1]), vec(ln1_g[1]), vec(ln1_b[1]),
                  conv_prologue=False, name="attn_out_ln1")
    xf = _mlp(xf, up, down, vec(ln2_g[1]), vec(ln2_b[1]), layer=1, name="mlp1")
    xf = _ple(xf, pf, proj, gate, layer=1, name="ple1")
    return xf.reshape(b_, s, d)
```

```python
import functools

import jax
import jax.numpy as jnp
from jax import lax
from jax.experimental import pallas as pl
from jax.experimental.pallas import tpu as pltpu

D_MODEL = 2048
DEPTH = 2
HEAD_DIM = 128
N_HEADS = D_MODEL // HEAD_DIM
DILATIONS = (1, 4, 16)
N_GROUPS = len(DILATIONS)
ATTN_BLOCK = 128
ATTN_TILE = ATTN_BLOCK * DILATIONS[-1]
CONV_WIDTH = 31
CONV_HALO = 32
D_FF = 4 * D_MODEL
PLE_DIM = 256
ROPE_THETA = 10000.0
LN_EPS = 1e-5
DEEPNORM_ALPHA = (2 * DEPTH) ** 0.25

VMEM_LIMIT_BYTES = 56 * 1024 * 1024
BF16 = jnp.bfloat16
F32 = jnp.float32


def _params(*semantics):
    return pltpu.CompilerParams(dimension_semantics=semantics, vmem_limit_bytes=VMEM_LIMIT_BYTES)


def _layer_norm(v, g, b):
    mu = jnp.mean(v, axis=-1, keepdims=True)
    c = v - mu
    var = jnp.mean(c * c, axis=-1, keepdims=True)
    return c * lax.rsqrt(var + LN_EPS) * g + b


def _mm(a, b):
    return jnp.dot(a, b, preferred_element_type=F32)


CONV_TM = 256
CONV_TN = 1024
CONV_ROWS = 16


def _glu_conv_kernel(x_ref, wa_ref, wg_ref, ba_ref, bg_ref, dw_ref, dwb_ref, o_ref, ubuf):
    i = pl.program_id(2)

    @pl.when(i == 0)
    def _():
        ubuf[pl.ds(0, CONV_HALO)] = jnp.zeros((CONV_HALO, 1, CONV_TN), F32)

    xb = x_ref[0].astype(BF16)
    a = _mm(xb, wa_ref[...]) + ba_ref[...]
    g = _mm(xb, wg_ref[...]) + bg_ref[...]
    ubuf[pl.ds(CONV_HALO, CONV_TM), 0, :] = a * jax.nn.sigmoid(g)

    taps = [dw_ref[k, 0] for k in range(CONV_WIDTH)]
    bias = dwb_ref[0, 0]
    first = CONV_HALO - (CONV_WIDTH - 1)

    def body(c, carry):
        t0 = c * CONV_ROWS
        acc = [bias] * CONV_ROWS
        for e in range(CONV_ROWS + CONV_WIDTH - 1):
            row = ubuf[t0 + first + e, 0]
            for r in range(max(0, e - CONV_WIDTH + 1), min(CONV_ROWS, e + 1)):
                acc[r] = acc[r] + taps[e - r] * row
        for r in range(CONV_ROWS):
            o_ref[0, t0 + r, 0] = acc[r]
        return carry

    lax.fori_loop(0, CONV_TM // CONV_ROWS, body, 0)
    ubuf[pl.ds(0, CONV_HALO)] = ubuf[pl.ds(CONV_TM, CONV_HALO)]


def _glu_conv(x, w_in, b_in, dw, dw_b):
    b_, s, d = x.shape
    nj = d // CONV_TN
    return pl.pallas_call(
        _glu_conv_kernel,
        grid=(nj, b_, s // CONV_TM),
        in_specs=[
            pl.BlockSpec((1, CONV_TM, d), lambda j, b, i: (b, i, 0)),
            pl.BlockSpec((d, CONV_TN), lambda j, b, i: (0, j)),
            pl.BlockSpec((d, CONV_TN), lambda j, b, i: (0, j + nj)),
            pl.BlockSpec((1, CONV_TN), lambda j, b, i: (0, j)),
            pl.BlockSpec((1, CONV_TN), lambda j, b, i: (0, j + nj)),
            pl.BlockSpec((CONV_WIDTH, 1, CONV_TN), lambda j, b, i: (0, 0, j)),
            pl.BlockSpec((1, 1, CONV_TN), lambda j, b, i: (0, 0, j)),
        ],
        out_specs=pl.BlockSpec((1, CONV_TM, 1, CONV_TN), lambda j, b, i: (b, i, 0, j)),
        out_shape=jax.ShapeDtypeStruct((b_, s, 1, d), F32),
        scratch_shapes=[pltpu.VMEM((CONV_HALO + CONV_TM, 1, CONV_TN), F32)],
        compiler_params=_params("arbitrary", "arbitrary", "arbitrary"),
        name="glu_conv",
    )(x, w_in, w_in, b_in, b_in, dw, dw_b)


ROW_TM = 512


def _proj_ln_kernel(act_ref, x_ref, w_ref, pg_ref, pb_ref, g_ref, b_ref, o_ref, *scratch, conv_prologue):
    if conv_prologue:
        a2d_ref, = scratch
        a2d_ref[...] = act_ref[:, 0, :]
        a = _layer_norm(a2d_ref[...], pg_ref[...], pb_ref[...])
        a = a * jax.nn.sigmoid(a)
    else:
        a = act_ref[...]
    mix = _mm(a.astype(BF16), w_ref[...])
    o_ref[...] = _layer_norm(DEEPNORM_ALPHA * x_ref[...] + mix, g_ref[...], b_ref[...])


def _proj_ln(act, x, w, pro_g, pro_b, g, b, *, conv_prologue, name):
    t, d = x.shape
    row = pl.BlockSpec((ROW_TM, d), lambda i: (i, 0))
    vec = pl.BlockSpec((1, d), lambda i: (0, 0))
    act_spec = pl.BlockSpec((ROW_TM, 1, d), lambda i: (i, 0, 0)) if conv_prologue else row
    return pl.pallas_call(
        functools.partial(_proj_ln_kernel, conv_prologue=conv_prologue),
        grid=(t // ROW_TM,),
        in_specs=[act_spec, row, pl.BlockSpec((d, d), lambda i: (0, 0)), vec, vec, vec, vec],
        out_specs=row,
        out_shape=jax.ShapeDtypeStruct((t, d), F32),
        scratch_shapes=[pltpu.VMEM((ROW_TM, d), F32)] if conv_prologue else [],
        compiler_params=_params("arbitrary"),
        name=name,
    )(act, x, w, pro_g, pro_b, g, b)


MLP_TM = 1024
MLP_TF = 512


def _mlp_kernel(x_ref, up_ref, down_ref, g_ref, b_ref, o_ref, xb_ref):
    k = pl.program_id(1)

    @pl.when(k == 0)
    def _():
        xb_ref[...] = x_ref[...].astype(BF16)
        o_ref[...] = jnp.zeros_like(o_ref)

    h = jnp.maximum(_mm(xb_ref[...], up_ref[...].astype(BF16)), 0.0)
    o_ref[...] += _mm((h * h).astype(BF16), down_ref[...].astype(BF16))

    @pl.when(k == pl.num_programs(1) - 1)
    def _():
        o_ref[...] = _layer_norm(DEEPNORM_ALPHA * x_ref[...] + o_ref[...], g_ref[...], b_ref[...])


def _mlp(x, up, down, g, b, *, layer, name):
    t, d = x.shape
    ff = up.shape[2]
    row = pl.BlockSpec((MLP_TM, d), lambda i, k: (i, 0))
    row_in = pl.BlockSpec((MLP_TM, d), lambda i, k: (i, 0), pipeline_mode=pl.Buffered(1))
    vec = pl.BlockSpec((1, d), lambda i, k: (0, 0))
    return pl.pallas_call(
        _mlp_kernel,
        grid=(t // MLP_TM, ff // MLP_TF),
        in_specs=[row_in, pl.BlockSpec((None, d, MLP_TF), lambda i, k: (layer, 0, k)),
                  pl.BlockSpec((None, MLP_TF, d), lambda i, k: (layer, k, 0)), vec, vec],
        out_specs=row,
        out_shape=jax.ShapeDtypeStruct((t, d), F32),
        scratch_shapes=[pltpu.VMEM((MLP_TM, d), BF16)],
        compiler_params=_params("arbitrary", "arbitrary"),
        name=name,
    )(x, up, down, g, b)


def _ple_kernel(x_ref, p_ref, proj_ref, gate_ref, o_ref):
    x = x_ref[...]
    emb = _mm(p_ref[...].astype(BF16), proj_ref[...])
    gate = jax.nn.sigmoid(_mm(x.astype(BF16), gate_ref[...]))
    o_ref[...] = x + emb * gate


def _ple(x, p, proj, gate, *, layer, name):
    t, d = x.shape
    row = pl.BlockSpec((ROW_TM, d), lambda i: (i, 0))
    return pl.pallas_call(
        _ple_kernel,
        grid=(t // ROW_TM,),
        in_specs=[row, pl.BlockSpec((None, ROW_TM, PLE_DIM), lambda i: (layer, i, 0)),
                  pl.BlockSpec((None, PLE_DIM, d), lambda i: (layer, 0, 0)),
                  pl.BlockSpec((None, d, d), lambda i: (layer, 0, 0))],
        out_specs=row,
        out_shape=jax.ShapeDtypeStruct((t, d), F32),
        compiler_params=_params("arbitrary"),
        name=name,
    )(x, p, proj, gate)


ROPE_TM = 512


def _rope_table_kernel(pos_ref, freq_ref, cos_ref, sin_ref):
    ang = pos_ref[...] * freq_ref[...]
    lane = lax.broadcasted_iota(jnp.int32, ang.shape, 1)
    cos_ref[...] = jnp.cos(ang)
    sin_ref[...] = jnp.where(lane < HEAD_DIM // 2, -1.0, 1.0) * jnp.sin(ang)


def _rope_tables(positions):
    t = positions.size
    half = HEAD_DIM // 2
    inv_freq = ROPE_THETA ** (-jnp.arange(half, dtype=F32) * (2.0 / HEAD_DIM))
    freq = jnp.concatenate([inv_freq, inv_freq]).reshape(1, HEAD_DIM)
    pos = positions.astype(F32).reshape(t, 1)
    tab = pl.BlockSpec((ROPE_TM, HEAD_DIM), lambda i: (i, 0))
    return pl.pallas_call(
        _rope_table_kernel,
        grid=(t // ROPE_TM,),
        in_specs=[pl.BlockSpec((ROPE_TM, 1), lambda i: (i, 0)), pl.BlockSpec((1, HEAD_DIM), lambda i: (0, 0))],
        out_specs=[tab, tab],
        out_shape=[jax.ShapeDtypeStruct((t, HEAD_DIM), F32)] * 2,
        compiler_params=_params("arbitrary"),
        name="rope_tables",
    )(pos, freq)


def _rotary(v, cos, sin_signed):
    return v * cos + pltpu.roll(v, HEAD_DIM // 2, axis=1) * sin_signed


HEADS_TM = 1024
HEADS_TN = 512
HEADS_PER_TILE = HEADS_TN // HEAD_DIM


def _heads_kernel(x_ref, w_ref, cos_ref, sin_ref, g_ref, b_ref, *refs, dilations, ln_prologue, n_rotary_tiles):
    out_refs = refs[:len(dilations)]
    xb_ref, stage_ref = refs[len(dilations):]
    j = pl.program_id(2)

    @pl.when(j == 0)
    def _():
        xv = x_ref[0]
        if ln_prologue:
            xv = _layer_norm(xv, g_ref[...], b_ref[...])
        xb_ref[...] = xv.astype(BF16)

    res = _mm(xb_ref[...], w_ref[...])

    def store(rotate):
        for hh in range(HEADS_PER_TILE):
            v = res[:, hh * HEAD_DIM:(hh + 1) * HEAD_DIM]
            if rotate:
                v = _rotary(v, cos_ref[...], sin_ref[...])
            if max(dilations) > 1:
                stage_ref[hh] = v
            for o_ref, dil in zip(out_refs, dilations):
                if dil == 1:
                    o_ref[0, hh] = v.astype(BF16)
                    continue
                for r in range(dil):
                    rows = stage_ref[hh, pl.ds(r, HEADS_TM // dil, stride=dil), :]
                    o_ref[0, hh, :, r * HEAD_DIM:(r + 1) * HEAD_DIM] = rows.astype(BF16)

    if n_rotary_tiles is None:
        store(True)
    else:
        pl.when(j < n_rotary_tiles)(functools.partial(store, True))
        pl.when(j >= n_rotary_tiles)(functools.partial(store, False))


def _heads_proj(x, w, cos, sin, g, b, *, n_out, col_tile_offset, dilations, ln_prologue, n_rotary_tiles, name):
    b_, s, d = x.shape
    per_b = s // HEADS_TM
    tab = pl.BlockSpec((HEADS_TM, HEAD_DIM), lambda bb, i, j: (bb * per_b + i, 0))
    vec = pl.BlockSpec((1, d), lambda bb, i, j: (0, 0))
    n_heads = n_out // HEAD_DIM
    return pl.pallas_call(
        functools.partial(_heads_kernel, dilations=dilations, ln_prologue=ln_prologue, n_rotary_tiles=n_rotary_tiles),
        grid=(b_, per_b, n_out // HEADS_TN),
        in_specs=[pl.BlockSpec((1, HEADS_TM, d), lambda bb, i, j: (bb, i, 0)),
                  pl.BlockSpec((d, HEADS_TN), lambda bb, i, j: (0, col_tile_offset + j)), tab, tab, vec, vec],
        out_specs=[pl.BlockSpec((1, HEADS_PER_TILE, HEADS_TM // dil, dil * HEAD_DIM), lambda bb, i, j: (bb, j, i, 0))
                   for dil in dilations],
        out_shape=[jax.ShapeDtypeStruct((b_, n_heads, s // dil, dil * HEAD_DIM), BF16) for dil in dilations],
        scratch_shapes=[pltpu.VMEM((HEADS_TM, d), BF16), pltpu.VMEM((HEADS_PER_TILE, HEADS_TM, HEAD_DIM), F32)],
        compiler_params=_params("arbitrary", "arbitrary", "arbitrary"),
        name=name,
    )(x, w, cos, sin, g, b)


MERGE_ROWS = 256


def _attn_kernel(*refs):
    group_refs = [refs[5 * g:5 * g + 5] for g in range(N_GROUPS)]
    o_ref, og_ref, lse_ref, bias_ref = refs[5 * N_GROUPS:]
    tile = pl.program_id(2)
    scale = HEAD_DIM ** -0.5
    nt_dims = (((1,), (1,)), ((), ()))

    qi = lax.broadcasted_iota(jnp.int32, (ATTN_BLOCK, 2 * ATTN_BLOCK), 0)
    kj = lax.broadcasted_iota(jnp.int32, (ATTN_BLOCK, 2 * ATTN_BLOCK), 1)
    bias = jnp.where(jnp.logical_and(kj >= qi, kj <= qi + ATTN_BLOCK), 0.0, -jnp.inf)
    bias_ref[0] = bias
    bias_ref[1] = jnp.where(jnp.logical_or(tile > 0, kj >= ATTN_BLOCK), bias, -jnp.inf)

    for g, (dil, (q_ref, kc_ref, kp_ref, vc_ref, vp_ref)) in enumerate(zip(DILATIONS, group_refs)):
        n_blocks = ATTN_TILE // (ATTN_BLOCK * dil)
        for r in range(dil):
            lanes = slice(r * HEAD_DIM, (r + 1) * HEAD_DIM)
            k_prev = kp_ref[0, 0, :, lanes]
            v_prev = vp_ref[0, 0, :, lanes]
            for n in range(n_blocks):
                rows = slice(n * ATTN_BLOCK, (n + 1) * ATTN_BLOCK)
                q = q_ref[0, 0, rows, lanes]
                k_cur = kc_ref[0, 0, rows, lanes]
                v_cur = vc_ref[0, 0, rows, lanes]
                keys = jnp.concatenate([k_prev, k_cur], axis=0)
                vals = jnp.concatenate([v_prev, v_cur], axis=0)
                s = lax.dot_general(q, keys, nt_dims, preferred_element_type=F32) * scale
                s = s + bias_ref[1 if n == 0 else 0]
                m = jnp.max(s, axis=1, keepdims=True)
                p = jnp.exp(s - m)
                l = jnp.sum(p, axis=1, keepdims=True)
                pv = _mm(p.astype(BF16), vals)
                if dil == 1:
                    dst = pl.ds(n * ATTN_BLOCK, ATTN_BLOCK)
                else:
                    dst = pl.ds(n * ATTN_BLOCK * dil + r, ATTN_BLOCK, stride=dil)
                og_ref[g, dst, :] = pv / l
                lse_ref[g, dst, :] = jnp.broadcast_to(m + jnp.log(l), (ATTN_BLOCK, HEAD_DIM))
                k_prev, v_prev = k_cur, v_cur

    for c in range(ATTN_TILE // MERGE_ROWS):
        rows = slice(c * MERGE_ROWS, (c + 1) * MERGE_ROWS)
        lses = [lse_ref[g, rows, :] for g in range(N_GROUPS)]
        top = functools.reduce(jnp.maximum, lses)
        wts = [jnp.exp(v - top) for v in lses]
        num = sum(w * og_ref[g, rows, :] for g, w in enumerate(wts))
        o_ref[0, rows, :] = (num / sum(wts)).astype(o_ref.dtype)


def _attention(q_views, kv_views):
    hd = HEAD_DIM
    b_ = q_views[0].shape[0]
    s = q_views[0].shape[2]
    n_tiles = s // ATTN_TILE
    operands, in_specs = [], []
    for dil, qv, kvv in zip(DILATIONS, q_views, kv_views):
        rows, width = ATTN_TILE // dil, dil * hd
        prev_per_tile = rows // ATTN_BLOCK

        def cur(head_offset):
            return pl.BlockSpec((1, 1, rows, width), lambda b, h, t, o=head_offset: (b, o + h, t, 0))

        def prev(head_offset):
            return pl.BlockSpec((1, 1, ATTN_BLOCK, width),
                                lambda b, h, t, o=head_offset, n=prev_per_tile: (b, o + h, jnp.maximum(n * t - 1, 0), 0))

        operands += [qv, kvv, kvv, kvv, kvv]
        in_specs += [cur(0), cur(0), prev(0), cur(N_HEADS), prev(N_HEADS)]

    return pl.pallas_call(
        _attn_kernel,
        grid=(b_, N_HEADS, n_tiles),
        in_specs=in_specs,
        out_specs=pl.BlockSpec((1, ATTN_TILE, hd), lambda b, h, t: (b, t, h)),
        out_shape=jax.ShapeDtypeStruct((b_, s, N_HEADS * hd), BF16),
        scratch_shapes=[pltpu.VMEM((N_GROUPS, ATTN_TILE, hd), F32), pltpu.VMEM((N_GROUPS, ATTN_TILE, hd), F32),
                        pltpu.VMEM((2, ATTN_BLOCK, 2 * ATTN_BLOCK), F32)],
        compiler_params=_params("arbitrary", "arbitrary", "arbitrary"),
        name="dilated_attention",
    )(*operands)


def kernel(x, p, positions, conv_w_in, conv_b_in, conv_dw, conv_dw_b, conv_ln_g, conv_ln_b, conv_w_out,
           kv_ln_g, kv_ln_b, w_kv, attn_w_q, attn_w_o, ln1_g, ln1_b, mlp_up, mlp_down, ln2_g, ln2_b,
           ple_proj, ple_gate):
    b_, s, d = x.shape
    t = b_ * s
    assert s % ATTN_TILE == 0 and d == D_MODEL

    def vec(v):
        return v.reshape(1, -1)

    def wb(w):
        return w.astype(BF16)

    cos, sin = _rope_tables(positions)

    up, down, proj, gate = mlp_up, mlp_down, wb(ple_proj), wb(ple_gate)
    pf = p.reshape(DEPTH, t, PLE_DIM)
    conv = _glu_conv(x, wb(conv_w_in[0]), vec(conv_b_in[0]), conv_dw[0], conv_dw_b[0].reshape(1, 1, d))
    xf = x.reshape(t, d)
    xf = _proj_ln(conv.reshape(t, 1, d), xf, wb(conv_w_out[0]), vec(conv_ln_g[0]), vec(conv_ln_b[0]),
                  vec(ln1_g[0]), vec(ln1_b[0]), conv_prologue=True, name="conv_out_ln1")
    xf = _mlp(xf, up, down, vec(ln2_g[0]), vec(ln2_b[0]), layer=0, name="mlp0")
    xf = _ple(xf, pf, proj, gate, layer=0, name="ple0")

    x3 = xf.reshape(b_, s, d)
    kv_views = _heads_proj(x3, wb(w_kv), cos, sin, vec(kv_ln_g), vec(kv_ln_b), n_out=2 * d, col_tile_offset=0,
                           dilations=DILATIONS, ln_prologue=True, n_rotary_tiles=d // HEADS_TN, name="kv_proj")
    w_q = wb(attn_w_q[0])
    q_views = [
        _heads_proj(x3, w_q, cos, sin, vec(kv_ln_g), vec(kv_ln_b), n_out=d, col_tile_offset=g * (d // HEADS_TN),
                    dilations=(dil,), ln_prologue=False, n_rotary_tiles=None, name=f"q_proj_d{dil}")[0]
        for g, dil in enumerate(DILATIONS)]
    o = _attention(q_views, kv_views)
    xf = _proj_ln(o.reshape(t, d), xf, wb(attn_w_o[0]), vec(ln1_g[1]), vec(ln1_b[1]), vec(ln1_g[1]), vec(ln1_b[1]),
                  conv_prologue=False, name="attn_out_ln1")
    xf = _mlp(xf, up, down, vec(ln2_g[1]), vec(ln2_b[1]), layer=1, name="mlp1")
    xf = _ple(xf, pf, proj, gate, layer=1, name="ple1")
    return xf.reshape(b_, s, d)
```

```python
import functools

import jax
import jax.numpy as jnp
from jax import lax
from jax.experimental import pallas as pl
from jax.experimental.pallas import tpu as pltpu

D_MODEL = 2048
DEPTH = 2
HEAD_DIM = 128
N_HEADS = D_MODEL // HEAD_DIM
DILATIONS = (1, 4, 16)
N_GROUPS = len(DILATIONS)
ATTN_BLOCK = 128
ATTN_TILE = ATTN_BLOCK * DILATIONS[-1]
CONV_WIDTH = 31
CONV_HALO = 32
D_FF = 4 * D_MODEL
PLE_DIM = 256
ROPE_THETA = 10000.0
LN_EPS = 1e-5
DEEPNORM_ALPHA = (2 * DEPTH) ** 0.25

VMEM_LIMIT_BYTES = 56 * 1024 * 1024
BF16 = jnp.bfloat16
F32 = jnp.float32


def _params(*semantics):
    return pltpu.CompilerParams(dimension_semantics=semantics, vmem_limit_bytes=VMEM_LIMIT_BYTES)


def _layer_norm(v, g, b):
    mu = jnp.mean(v, axis=-1, keepdims=True)
    c = v - mu
    var = jnp.mean(c * c, axis=-1, keepdims=True)
    return c * lax.rsqrt(var + LN_EPS) * g + b


def _mm(a, b):
    return jnp.dot(a, b, preferred_element_type=F32)


CONV_TM = 256
CONV_TN = 1024
CONV_ROWS = 16


def _glu_conv_kernel(x_ref, wa_ref, wg_ref, ba_ref, bg_ref, dw_ref, dwb_ref, up_ref, down_ref,
                     o_ref, up_bf_ref, down_bf_ref, ubuf):
    i = pl.program_id(2)

    up_bf_ref[...] = up_ref[...].astype(BF16)
    down_bf_ref[...] = down_ref[...].astype(BF16)

    @pl.when(i == 0)
    def _():
        ubuf[pl.ds(0, CONV_HALO)] = jnp.zeros((CONV_HALO, 1, CONV_TN), F32)

    xb = x_ref[0].astype(BF16)
    a = _mm(xb, wa_ref[...]) + ba_ref[...]
    g = _mm(xb, wg_ref[...]) + bg_ref[...]
    ubuf[pl.ds(CONV_HALO, CONV_TM), 0, :] = a * jax.nn.sigmoid(g)

    taps = [dw_ref[k, 0] for k in range(CONV_WIDTH)]
    bias = dwb_ref[0, 0]
    first = CONV_HALO - (CONV_WIDTH - 1)

    def body(c, carry):
        t0 = c * CONV_ROWS
        acc = [bias] * CONV_ROWS
        for e in range(CONV_ROWS + CONV_WIDTH - 1):
            row = ubuf[t0 + first + e, 0]
            for r in range(max(0, e - CONV_WIDTH + 1), min(CONV_ROWS, e + 1)):
                acc[r] = acc[r] + taps[e - r] * row
        for r in range(CONV_ROWS):
            o_ref[0, t0 + r, 0] = acc[r]
        return carry

    lax.fori_loop(0, CONV_TM // CONV_ROWS, body, 0)
    ubuf[pl.ds(0, CONV_HALO)] = ubuf[pl.ds(CONV_TM, CONV_HALO)]


def _glu_conv(x, w_in, b_in, dw, dw_b, mlp_up, mlp_down):
    b_, s, d = x.shape
    nj = d // CONV_TN
    ni = s // CONV_TM
    steps = nj * b_ * ni
    up2d = mlp_up.reshape(-1, mlp_up.shape[-1])
    down2d = mlp_down.reshape(-1, mlp_down.shape[-1])
    up_rows, down_rows = up2d.shape[0] // steps, down2d.shape[0] // steps
    assert up_rows * steps == up2d.shape[0] and down_rows * steps == down2d.shape[0]

    def slab(rows, width):
        return pl.BlockSpec((rows, width), lambda j, b, i: ((j * b_ + b) * ni + i, 0))

    conv, up_bf, down_bf = pl.pallas_call(
        _glu_conv_kernel,
        grid=(nj, b_, ni),
        in_specs=[
            pl.BlockSpec((1, CONV_TM, d), lambda j, b, i: (b, i, 0)),
            pl.BlockSpec((d, CONV_TN), lambda j, b, i: (0, j)),
            pl.BlockSpec((d, CONV_TN), lambda j, b, i: (0, j + nj)),
            pl.BlockSpec((1, CONV_TN), lambda j, b, i: (0, j)),
            pl.BlockSpec((1, CONV_TN), lambda j, b, i: (0, j + nj)),
            pl.BlockSpec((CONV_WIDTH, 1, CONV_TN), lambda j, b, i: (0, 0, j)),
            pl.BlockSpec((1, 1, CONV_TN), lambda j, b, i: (0, 0, j)),
            slab(up_rows, up2d.shape[1]),
            slab(down_rows, down2d.shape[1]),
        ],
        out_specs=[pl.BlockSpec((1, CONV_TM, 1, CONV_TN), lambda j, b, i: (b, i, 0, j)),
                   slab(up_rows, up2d.shape[1]), slab(down_rows, down2d.shape[1])],
        out_shape=[jax.ShapeDtypeStruct((b_, s, 1, d), F32), jax.ShapeDtypeStruct(up2d.shape, BF16),
                   jax.ShapeDtypeStruct(down2d.shape, BF16)],
        scratch_shapes=[pltpu.VMEM((CONV_HALO + CONV_TM, 1, CONV_TN), F32)],
        compiler_params=_params("arbitrary", "arbitrary", "arbitrary"),
        name="glu_conv",
    )(x, w_in, w_in, b_in, b_in, dw, dw_b, up2d, down2d)
    return conv, up_bf.reshape(mlp_up.shape), down_bf.reshape(mlp_down.shape)


ROW_TM = 512
ROW_SPLIT = 2


def _proj_ln_kernel(act_ref, x_ref, w_ref, pg_ref, pb_ref, g_ref, b_ref, o_ref, *scratch, conv_prologue):
    for h in range(ROW_SPLIT):
        rows = pl.ds(h * (ROW_TM // ROW_SPLIT), ROW_TM // ROW_SPLIT)
        if conv_prologue:
            a2d_ref, = scratch
            a2d_ref[rows, :] = act_ref[rows, 0, :]
            a = _layer_norm(a2d_ref[rows, :], pg_ref[...], pb_ref[...])
            a = a * jax.nn.sigmoid(a)
        else:
            a = act_ref[rows, :]
        mix = _mm(a.astype(BF16), w_ref[...])
        o_ref[rows, :] = _layer_norm(DEEPNORM_ALPHA * x_ref[rows, :] + mix, g_ref[...], b_ref[...])


def _proj_ln(act, x, w, pro_g, pro_b, g, b, *, conv_prologue, name):
    t, d = x.shape
    row = pl.BlockSpec((ROW_TM, d), lambda i: (i, 0))
    vec = pl.BlockSpec((1, d), lambda i: (0, 0))
    act_spec = pl.BlockSpec((ROW_TM, 1, d), lambda i: (i, 0, 0)) if conv_prologue else row
    return pl.pallas_call(
        functools.partial(_proj_ln_kernel, conv_prologue=conv_prologue),
        grid=(t // ROW_TM,),
        in_specs=[act_spec, row, pl.BlockSpec((d, d), lambda i: (0, 0)), vec, vec, vec, vec],
        out_specs=row,
        out_shape=jax.ShapeDtypeStruct((t, d), F32),
        scratch_shapes=[pltpu.VMEM((ROW_TM, d), F32)] if conv_prologue else [],
        compiler_params=_params("arbitrary"),
        name=name,
    )(act, x, w, pro_g, pro_b, g, b)


MLP_TM = 1024
MLP_TF = 512


def _mlp_kernel(x_ref, up_ref, down_ref, g_ref, b_ref, o_ref, xb_ref):
    k = pl.program_id(1)

    @pl.when(k == 0)
    def _():
        xb_ref[...] = x_ref[...].astype(BF16)
        o_ref[...] = jnp.zeros_like(o_ref)

    h = jnp.maximum(_mm(xb_ref[...], up_ref[...]), 0.0)
    o_ref[...] += _mm((h * h).astype(BF16), down_ref[...])

    @pl.when(k == pl.num_programs(1) - 1)
    def _():
        o_ref[...] = _layer_norm(DEEPNORM_ALPHA * x_ref[...] + o_ref[...], g_ref[...], b_ref[...])


def _mlp(x, up, down, g, b, *, layer, name):
    t, d = x.shape
    ff = up.shape[2]
    row = pl.BlockSpec((MLP_TM, d), lambda i, k: (i, 0))
    vec = pl.BlockSpec((1, d), lambda i, k: (0, 0))
    return pl.pallas_call(
        _mlp_kernel,
        grid=(t // MLP_TM, ff // MLP_TF),
        in_specs=[row, pl.BlockSpec((None, d, MLP_TF), lambda i, k: (layer, 0, k)),
                  pl.BlockSpec((None, MLP_TF, d), lambda i, k: (layer, k, 0)), vec, vec],
        out_specs=row,
        out_shape=jax.ShapeDtypeStruct((t, d), F32),
        scratch_shapes=[pltpu.VMEM((MLP_TM, d), BF16)],
        compiler_params=_params("arbitrary", "arbitrary"),
        name=name,
    )(x, up, down, g, b)


def _ple_kernel(x_ref, p_ref, proj_ref, gate_ref, o_ref):
    for h in range(ROW_SPLIT):
        rows = pl.ds(h * (ROW_TM // ROW_SPLIT), ROW_TM // ROW_SPLIT)
        x = x_ref[rows, :]
        emb = _mm(p_ref[rows, :].astype(BF16), proj_ref[...])
        gate = jax.nn.sigmoid(_mm(x.astype(BF16), gate_ref[...]))
        o_ref[rows, :] = x + emb * gate


def _ple(x, p, proj, gate, *, layer, name):
    t, d = x.shape
    row = pl.BlockSpec((ROW_TM, d), lambda i: (i, 0))
    return pl.pallas_call(
        _ple_kernel,
        grid=(t // ROW_TM,),
        in_specs=[row, pl.BlockSpec((None, ROW_TM, PLE_DIM), lambda i: (layer, i, 0)),
                  pl.BlockSpec((None, PLE_DIM, d), lambda i: (layer, 0, 0)),
                  pl.BlockSpec((None, d, d), lambda i: (layer, 0, 0))],
        out_specs=row,
        out_shape=jax.ShapeDtypeStruct((t, d), F32),
        compiler_params=_params("arbitrary"),
        name=name,
    )(x, p, proj, gate)


ROPE_TM = 512


def _rope_table_kernel(pos_ref, freq_ref, cos_ref, sin_ref):
    ang = pos_ref[...] * freq_ref[...]
    lane = lax.broadcasted_iota(jnp.int32, ang.shape, 1)
    cos_ref[...] = jnp.cos(ang)
    sin_ref[...] = jnp.where(lane < HEAD_DIM // 2, -1.0, 1.0) * jnp.sin(ang)


def _rope_tables(positions):
    t = positions.size
    half = HEAD_DIM // 2
    inv_freq = ROPE_THETA ** (-jnp.arange(half, dtype=F32) * (2.0 / HEAD_DIM))
    freq = jnp.concatenate([inv_freq, inv_freq]).reshape(1, HEAD_DIM)
    pos = positions.astype(F32).reshape(t, 1)
    tab = pl.BlockSpec((ROPE_TM, HEAD_DIM), lambda i: (i, 0))
    return pl.pallas_call(
        _rope_table_kernel,
        grid=(t // ROPE_TM,),
        in_specs=[pl.BlockSpec((ROPE_TM, 1), lambda i: (i, 0)), pl.BlockSpec((1, HEAD_DIM), lambda i: (0, 0))],
        out_specs=[tab, tab],
        out_shape=[jax.ShapeDtypeStruct((t, HEAD_DIM), F32)] * 2,
        compiler_params=_params("arbitrary"),
        name="rope_tables",
    )(pos, freq)


def _rotary(v, cos, sin_signed):
    return v * cos + pltpu.roll(v, HEAD_DIM // 2, axis=1) * sin_signed


HEADS_TM = 1024
HEADS_TN = 512
HEADS_PER_TILE = HEADS_TN // HEAD_DIM
HEADS_CHUNK = 256
HEADS_SUB_M = 256


def _heads_kernel(x_ref, w_ref, cos_ref, sin_ref, g_ref, b_ref, *refs, dilations, ln_prologue, n_rotary_tiles):
    assert set(dilations) <= {1, 4, 16}
    outs = dict(zip(dilations, refs[:len(dilations)]))
    xb_ref, stage_ref, stage2_ref = refs[len(dilations):]
    quarter = HEADS_SUB_M // 4
    j = pl.program_id(2)

    @pl.when(j == 0)
    def _():
        xv = x_ref[0]
        if ln_prologue:
            xv = _layer_norm(xv, g_ref[...], b_ref[...])
        xb_ref[...] = xv.astype(BF16)

    cos, sin = cos_ref[...], sin_ref[...]
    if n_rotary_tiles is not None:
        rotate = j < n_rotary_tiles
        cos = jnp.where(rotate, cos, 1.0)
        sin = jnp.where(rotate, sin, 0.0)

    for c in range(HEADS_TN // HEADS_CHUNK):
        for m0 in range(0, HEADS_TM, HEADS_SUB_M):
            res = _mm(xb_ref[pl.ds(m0, HEADS_SUB_M), :], w_ref[:, c * HEADS_CHUNK:(c + 1) * HEADS_CHUNK])
            for ch in range(HEADS_CHUNK // HEAD_DIM):
                hh = c * (HEADS_CHUNK // HEAD_DIM) + ch
                v = _rotary(res[:, ch * HEAD_DIM:(ch + 1) * HEAD_DIM],
                            cos[m0:m0 + HEADS_SUB_M], sin[m0:m0 + HEADS_SUB_M])
                if 1 in outs:
                    outs[1][0, hh, pl.ds(m0, HEADS_SUB_M), :] = v.astype(BF16)
                if max(dilations) == 1:
                    continue
                stage_ref[hh, pl.ds(m0, HEADS_SUB_M), :] = v
                for r4 in range(4):
                    part = stage_ref[hh, pl.ds(m0 + r4, quarter, stride=4), :]
                    if 4 in outs:
                        outs[4][0, hh, pl.ds(m0 // 4, quarter), r4 * HEAD_DIM:(r4 + 1) * HEAD_DIM] = part.astype(BF16)
                    if 16 not in outs:
                        continue
                    stage2_ref[hh, pl.ds(m0 + r4 * quarter, quarter), :] = part
                    for q4 in range(4):
                        r16 = 4 * q4 + r4
                        sub = stage2_ref[hh, pl.ds(m0 + r4 * quarter + q4, quarter // 4, stride=4), :]
                        outs[16][0, hh, pl.ds(m0 // 16, quarter // 4), r16 * HEAD_DIM:(r16 + 1) * HEAD_DIM] = (
                            sub.astype(BF16))


def _heads_proj(x, w, cos, sin, g, b, *, n_out, col_tile_offset, dilations, ln_prologue, n_rotary_tiles, name):
    b_, s, d = x.shape
    per_b = s // HEADS_TM
    tab = pl.BlockSpec((HEADS_TM, HEAD_DIM), lambda bb, i, j: (bb * per_b + i, 0))
    vec = pl.BlockSpec((1, d), lambda bb, i, j: (0, 0))
    n_heads = n_out // HEAD_DIM
    return pl.pallas_call(
        functools.partial(_heads_kernel, dilations=dilations, ln_prologue=ln_prologue, n_rotary_tiles=n_rotary_tiles),
        grid=(b_, per_b, n_out // HEADS_TN),
        in_specs=[pl.BlockSpec((1, HEADS_TM, d), lambda bb, i, j: (bb, i, 0)),
                  pl.BlockSpec((d, HEADS_TN), lambda bb, i, j: (0, col_tile_offset + j)), tab, tab, vec, vec],
        out_specs=[pl.BlockSpec((1, HEADS_PER_TILE, HEADS_TM // dil, dil * HEAD_DIM), lambda bb, i, j: (bb, j, i, 0))
                   for dil in dilations],
        out_shape=[jax.ShapeDtypeStruct((b_, n_heads, s // dil, dil * HEAD_DIM), BF16) for dil in dilations],
        scratch_shapes=[pltpu.VMEM((HEADS_TM, d), BF16)]
        + [pltpu.VMEM((HEADS_PER_TILE, HEADS_TM, HEAD_DIM), F32)] * 2,
        compiler_params=_params("arbitrary", "arbitrary", "arbitrary"),
        name=name,
    )(x, w, cos, sin, g, b)


MERGE_ROWS = 256
LOG2_E = 1.4426950408889634


def _attn_kernel(*refs):
    group_refs = [refs[5 * g:5 * g + 5] for g in range(N_GROUPS)]
    o_ref, og_ref, lse_ref, bias_ref = refs[5 * N_GROUPS:]
    tile = pl.program_id(2)
    scale = HEAD_DIM ** -0.5
    nt_dims = (((1,), (1,)), ((), ()))

    qi = lax.broadcasted_iota(jnp.int32, (ATTN_BLOCK, 2 * ATTN_BLOCK), 0)
    kj = lax.broadcasted_iota(jnp.int32, (ATTN_BLOCK, 2 * ATTN_BLOCK), 1)
    bias = jnp.where(jnp.logical_and(kj >= qi, kj <= qi + ATTN_BLOCK), 0.0, -jnp.inf)
    bias_ref[0] = bias
    bias_ref[1] = jnp.where(jnp.logical_or(tile > 0, kj >= ATTN_BLOCK), bias, -jnp.inf)

    for g, (dil, (q_ref, kc_ref, kp_ref, vc_ref, vp_ref)) in enumerate(zip(DILATIONS, group_refs)):
        n_blocks = ATTN_TILE // (ATTN_BLOCK * dil)
        for r in range(dil):
            lanes = slice(r * HEAD_DIM, (r + 1) * HEAD_DIM)
            k_prev = kp_ref[0, 0, :, lanes]
            v_prev = vp_ref[0, 0, :, lanes]
            for n in range(n_blocks):
                rows = slice(n * ATTN_BLOCK, (n + 1) * ATTN_BLOCK)
                q = q_ref[0, 0, rows, lanes]
                k_cur = kc_ref[0, 0, rows, lanes]
                v_cur = vc_ref[0, 0, rows, lanes]
                keys = jnp.concatenate([k_prev, k_cur], axis=0)
                vals = jnp.concatenate([v_prev, v_cur], axis=0)
                s = lax.dot_general(q, keys, nt_dims, preferred_element_type=F32) + bias_ref[1 if n == 0 else 0]
                m = jnp.max(s, axis=1, keepdims=True)
                p = jnp.exp2((s - m) * (scale * LOG2_E))
                l = jnp.sum(p, axis=1, keepdims=True)
                pv = _mm(p.astype(BF16), vals)
                if dil == 1:
                    dst = pl.ds(n * ATTN_BLOCK, ATTN_BLOCK)
                else:
                    dst = pl.ds(n * ATTN_BLOCK * dil + r, ATTN_BLOCK, stride=dil)
                og_ref[g, dst, :] = pv / l
                lse_ref[g, dst, :] = jnp.broadcast_to(m * scale + jnp.log(l), (ATTN_BLOCK, HEAD_DIM))
                k_prev, v_prev = k_cur, v_cur

    for c in range(ATTN_TILE // MERGE_ROWS):
        rows = slice(c * MERGE_ROWS, (c + 1) * MERGE_ROWS)
        lses = [lse_ref[g, rows, :] for g in range(N_GROUPS)]
        top = functools.reduce(jnp.maximum, lses)
        wts = [jnp.exp(v - top) for v in lses]
        num = sum(w * og_ref[g, rows, :] for g, w in enumerate(wts))
        o_ref[0, rows, :] = (num / sum(wts)).astype(o_ref.dtype)


def _attention(q_views, kv_views):
    hd = HEAD_DIM
    b_ = q_views[0].shape[0]
    s = q_views[0].shape[2]
    n_tiles = s // ATTN_TILE
    operands, in_specs = [], []
    for dil, qv, kvv in zip(DILATIONS, q_views, kv_views):
        rows, width = ATTN_TILE // dil, dil * hd
        prev_per_tile = rows // ATTN_BLOCK

        def cur(head_offset):
            return pl.BlockSpec((1, 1, rows, width), lambda b, h, t, o=head_offset: (b, o + h, t, 0))

        def prev(head_offset):
            return pl.BlockSpec((1, 1, ATTN_BLOCK, width),
                                lambda b, h, t, o=head_offset, n=prev_per_tile: (b, o + h, jnp.maximum(n * t - 1, 0), 0))

        operands += [qv, kvv, kvv, kvv, kvv]
        in_specs += [cur(0), cur(0), prev(0), cur(N_HEADS), prev(N_HEADS)]

    return pl.pallas_call(
        _attn_kernel,
        grid=(b_, N_HEADS, n_tiles),
        in_specs=in_specs,
        out_specs=pl.BlockSpec((1, ATTN_TILE, hd), lambda b, h, t: (b, t, h)),
        out_shape=jax.ShapeDtypeStruct((b_, s, N_HEADS * hd), BF16),
        scratch_shapes=[pltpu.VMEM((N_GROUPS, ATTN_TILE, hd), F32), pltpu.VMEM((N_GROUPS, ATTN_TILE, hd), F32),
                        pltpu.VMEM((2, ATTN_BLOCK, 2 * ATTN_BLOCK), F32)],
        compiler_params=_params("arbitrary", "arbitrary", "arbitrary"),
        name="dilated_attention",
    )(*operands)


def kernel(x, p, positions, conv_w_in, conv_b_in, conv_dw, conv_dw_b, conv_ln_g, conv_ln_b, conv_w_out,
           kv_ln_g, kv_ln_b, w_kv, attn_w_q, attn_w_o, ln1_g, ln1_b, mlp_up, mlp_down, ln2_g, ln2_b,
           ple_proj, ple_gate):
    b_, s, d = x.shape
    t = b_ * s
    assert s % ATTN_TILE == 0 and d == D_MODEL

    def vec(v):
        return v.reshape(1, -1)

    def wb(w):
        return w.astype(BF16)

    cos, sin = _rope_tables(positions)

    proj, gate = wb(ple_proj), wb(ple_gate)
    pf = p.reshape(DEPTH, t, PLE_DIM)
    conv, up, down = _glu_conv(x, wb(conv_w_in[0]), vec(conv_b_in[0]), conv_dw[0], conv_dw_b[0].reshape(1, 1, d),
                               mlp_up, mlp_down)
    xf = x.reshape(t, d)
    xf = _proj_ln(conv.reshape(t, 1, d), xf, wb(conv_w_out[0]), vec(conv_ln_g[0]), vec(conv_ln_b[0]),
                  vec(ln1_g[0]), vec(ln1_b[0]), conv_prologue=True, name="conv_out_ln1")
    xf = _mlp(xf, up, down, vec(ln2_g[0]), vec(ln2_b[0]), layer=0, name="mlp0")
    xf = _ple(xf, pf, proj, gate, layer=0, name="ple0")

    x3 = xf.reshape(b_, s, d)
    kv_views = _heads_proj(x3, wb(w_kv), cos, sin, vec(kv_ln_g), vec(kv_ln_b), n_out=2 * d, col_tile_offset=0,
                           dilations=DILATIONS, ln_prologue=True, n_rotary_tiles=d // HEADS_TN, name="kv_proj")
    w_q = wb(attn_w_q[0])
    q_views = [
        _heads_proj(x3, w_q, cos, sin, vec(kv_ln_g), vec(kv_ln_b), n_out=d, col_tile_offset=g * (d // HEADS_TN),
                    dilations=(dil,), ln_prologue=False, n_rotary_tiles=None, name=f"q_proj_d{dil}")[0]
        for g, dil in enumerate(DILATIONS)]
    o = _attention(q_views, kv_views)
    xf = _proj_ln(o.reshape(t, d), xf, wb(attn_w_o[0]), vec(ln1_g[1]), vec(ln1_b[1]), vec(ln1_g[1]), vec(ln1_b[1]),
                  conv_prologue=False, name="attn_out_ln1")
    xf = _mlp(xf, up, down, vec(ln2_g[1]), vec(ln2_b[1]), layer=1, name="mlp1")
    xf = _ple(xf, pf, proj, gate, layer=1, name="ple1")
    return xf.reshape(b_, s, d)
```

```python
import functools

import jax
import jax.numpy as jnp
from jax import lax
from jax.experimental import pallas as pl
from jax.experimental.pallas import tpu as pltpu

D_MODEL = 2048
DEPTH = 2
HEAD_DIM = 128
N_HEADS = D_MODEL // HEAD_DIM
DILATIONS = (1, 4, 16)
N_GROUPS = len(DILATIONS)
ATTN_BLOCK = 128
ATTN_TILE = ATTN_BLOCK * DILATIONS[-1]
CONV_WIDTH = 31
CONV_HALO = 32
D_FF = 4 * D_MODEL
PLE_DIM = 256
ROPE_THETA = 10000.0
LN_EPS = 1e-5
DEEPNORM_ALPHA = (2 * DEPTH) ** 0.25

VMEM_LIMIT_BYTES = 56 * 1024 * 1024
BF16_SUBLANES = 16
BF16 = jnp.bfloat16
F32 = jnp.float32


def _params(*semantics):
    return pltpu.CompilerParams(dimension_semantics=semantics, vmem_limit_bytes=VMEM_LIMIT_BYTES)


def _layer_norm(v, g, b):
    mu = jnp.mean(v, axis=-1, keepdims=True)
    c = v - mu
    var = jnp.mean(c * c, axis=-1, keepdims=True)
    return c * lax.rsqrt(var + LN_EPS) * g + b


def _mm(a, b):
    return jnp.dot(a, b, preferred_element_type=F32)


def _split(refs, *sizes):
    out, at = [], 0
    for n in sizes:
        out.append(refs[at:at + n])
        at += n
    assert at == len(refs)
    return out


class _Riders:
    def __init__(self, arrays, steps, step_of_grid):
        self.shapes = [a.shape for a in arrays]
        self.operands = [a.reshape(-1, a.shape[-1]) for a in arrays]
        self.specs, self.out_shapes = [], []
        for a in self.operands:
            rows, width = a.shape[0] // steps, a.shape[1]
            assert rows * steps == a.shape[0] and rows % BF16_SUBLANES == 0
            self.specs.append(pl.BlockSpec((rows, width), lambda *g: (step_of_grid(*g), 0)))
            self.out_shapes.append(jax.ShapeDtypeStruct(a.shape, BF16))

    def restore(self, rounded):
        return [r.reshape(s) for r, s in zip(rounded, self.shapes)]


def _round_riders(in_refs, out_refs):
    for src, dst in zip(in_refs, out_refs):
        dst[...] = src[...].astype(BF16)


CONV_TM = 256
CONV_TN = 1024
CONV_ROWS = 16


def _glu_conv_kernel(x_ref, wa_ref, wg_ref, ba_ref, bg_ref, dw_ref, dwb_ref, *refs, n_riders):
    rider_in, (o_ref,), rider_out, (ubuf,) = _split(refs, n_riders, 1, n_riders, 1)
    _round_riders(rider_in, rider_out)
    i = pl.program_id(2)

    @pl.when(i == 0)
    def _():
        ubuf[pl.ds(0, CONV_HALO)] = jnp.zeros((CONV_HALO, 1, CONV_TN), F32)

    xb = x_ref[0].astype(BF16)
    a = _mm(xb, wa_ref[...]) + ba_ref[...]
    g = _mm(xb, wg_ref[...]) + bg_ref[...]
    ubuf[pl.ds(CONV_HALO, CONV_TM), 0, :] = a * jax.nn.sigmoid(g)

    taps = [dw_ref[k, 0] for k in range(CONV_WIDTH)]
    bias = dwb_ref[0, 0]
    first = CONV_HALO - (CONV_WIDTH - 1)

    def body(c, carry):
        t0 = c * CONV_ROWS
        acc = [bias] * CONV_ROWS
        for e in range(CONV_ROWS + CONV_WIDTH - 1):
            row = ubuf[t0 + first + e, 0]
            for r in range(max(0, e - CONV_WIDTH + 1), min(CONV_ROWS, e + 1)):
                acc[r] = acc[r] + taps[e - r] * row
        for r in range(CONV_ROWS):
            o_ref[0, t0 + r, 0] = acc[r]
        return carry

    lax.fori_loop(0, CONV_TM // CONV_ROWS, body, 0)
    ubuf[pl.ds(0, CONV_HALO)] = ubuf[pl.ds(CONV_TM, CONV_HALO)]


def _glu_conv(x, w_in, b_in, dw, dw_b, riders):
    b_, s, d = x.shape
    nj = d // CONV_TN
    ni = s // CONV_TM
    rider = _Riders(riders, nj * b_ * ni, lambda j, b, i: (j * b_ + b) * ni + i)
    conv, *rounded = pl.pallas_call(
        functools.partial(_glu_conv_kernel, n_riders=len(riders)),
        grid=(nj, b_, ni),
        in_specs=[
            pl.BlockSpec((1, CONV_TM, d), lambda j, b, i: (b, i, 0)),
            pl.BlockSpec((d, CONV_TN), lambda j, b, i: (0, j)),
            pl.BlockSpec((d, CONV_TN), lambda j, b, i: (0, j + nj)),
            pl.BlockSpec((1, CONV_TN), lambda j, b, i: (0, j)),
            pl.BlockSpec((1, CONV_TN), lambda j, b, i: (0, j + nj)),
            pl.BlockSpec((CONV_WIDTH, 1, CONV_TN), lambda j, b, i: (0, 0, j)),
            pl.BlockSpec((1, 1, CONV_TN), lambda j, b, i: (0, 0, j)),
            *rider.specs,
        ],
        out_specs=[pl.BlockSpec((1, CONV_TM, 1, CONV_TN), lambda j, b, i: (b, i, 0, j)), *rider.specs],
        out_shape=[jax.ShapeDtypeStruct((b_, s, 1, d), F32), *rider.out_shapes],
        scratch_shapes=[pltpu.VMEM((CONV_HALO + CONV_TM, 1, CONV_TN), F32)],
        compiler_params=_params("arbitrary", "arbitrary", "arbitrary"),
        name="glu_conv",
    )(x, w_in, w_in, b_in, b_in, dw, dw_b, *rider.operands)
    return conv, rider.restore(rounded)


ROW_TM = 512
ROW_SPLIT = 2


def _proj_ln_kernel(act_ref, x_ref, w_ref, pg_ref, pb_ref, g_ref, b_ref, o_ref, *scratch, conv_prologue):
    for h in range(ROW_SPLIT):
        rows = pl.ds(h * (ROW_TM // ROW_SPLIT), ROW_TM // ROW_SPLIT)
        if conv_prologue:
            a2d_ref, = scratch
            a2d_ref[rows, :] = act_ref[rows, 0, :]
            a = _layer_norm(a2d_ref[rows, :], pg_ref[...], pb_ref[...])
            a = a * jax.nn.sigmoid(a)
        else:
            a = act_ref[rows, :]
        mix = _mm(a.astype(BF16), w_ref[...])
        o_ref[rows, :] = _layer_norm(DEEPNORM_ALPHA * x_ref[rows, :] + mix, g_ref[...], b_ref[...])


def _proj_ln(act, x, w, pro_g, pro_b, g, b, *, conv_prologue, name):
    t, d = x.shape
    row = pl.BlockSpec((ROW_TM, d), lambda i: (i, 0))
    vec = pl.BlockSpec((1, d), lambda i: (0, 0))
    act_spec = pl.BlockSpec((ROW_TM, 1, d), lambda i: (i, 0, 0)) if conv_prologue else row
    return pl.pallas_call(
        functools.partial(_proj_ln_kernel, conv_prologue=conv_prologue),
        grid=(t // ROW_TM,),
        in_specs=[act_spec, row, pl.BlockSpec((d, d), lambda i: (0, 0)), vec, vec, vec, vec],
        out_specs=row,
        out_shape=jax.ShapeDtypeStruct((t, d), F32),
        scratch_shapes=[pltpu.VMEM((ROW_TM, d), F32)] if conv_prologue else [],
        compiler_params=_params("arbitrary"),
        name=name,
    )(act, x, w, pro_g, pro_b, g, b)


MLP_TM = 1024
MLP_TF = 512


def _mlp_kernel(x_ref, up_ref, down_ref, g_ref, b_ref, *refs, n_riders):
    rider_in, (o_ref,), rider_out, (xb_ref,) = _split(refs, n_riders, 1, n_riders, 1)
    _round_riders(rider_in, rider_out)
    k = pl.program_id(1)

    @pl.when(k == 0)
    def _():
        xb_ref[...] = x_ref[...].astype(BF16)
        o_ref[...] = jnp.zeros_like(o_ref)

    h = jnp.maximum(_mm(xb_ref[...], up_ref[...]), 0.0)
    o_ref[...] += _mm((h * h).astype(BF16), down_ref[...])

    @pl.when(k == pl.num_programs(1) - 1)
    def _():
        o_ref[...] = _layer_norm(DEEPNORM_ALPHA * x_ref[...] + o_ref[...], g_ref[...], b_ref[...])


def _mlp(x, up, down, g, b, *, layer, name, riders=()):
    t, d = x.shape
    ff = up.shape[2]
    nk = ff // MLP_TF
    row = pl.BlockSpec((MLP_TM, d), lambda i, k: (i, 0))
    vec = pl.BlockSpec((1, d), lambda i, k: (0, 0))
    rider = _Riders(riders, (t // MLP_TM) * nk, lambda i, k: i * nk + k)
    out, *rounded = pl.pallas_call(
        functools.partial(_mlp_kernel, n_riders=len(riders)),
        grid=(t // MLP_TM, nk),
        in_specs=[pl.BlockSpec((MLP_TM, d), lambda i, k: (i, 0), pipeline_mode=pl.Buffered(1)),
                  pl.BlockSpec((None, d, MLP_TF), lambda i, k: (layer, 0, k)),
                  pl.BlockSpec((None, MLP_TF, d), lambda i, k: (layer, k, 0)), vec, vec, *rider.specs],
        out_specs=[row, *rider.specs],
        out_shape=[jax.ShapeDtypeStruct((t, d), F32), *rider.out_shapes],
        scratch_shapes=[pltpu.VMEM((MLP_TM, d), BF16)],
        compiler_params=_params("arbitrary", "arbitrary"),
        name=name,
    )(x, up, down, g, b, *rider.operands)
    return out, rider.restore(rounded)


def _ple_kernel(x_ref, p_ref, proj_ref, gate_ref, o_ref):
    for h in range(ROW_SPLIT):
        rows = pl.ds(h * (ROW_TM // ROW_SPLIT), ROW_TM // ROW_SPLIT)
        x = x_ref[rows, :]
        emb = _mm(p_ref[rows, :].astype(BF16), proj_ref[...])
        gate = jax.nn.sigmoid(_mm(x.astype(BF16), gate_ref[...]))
        o_ref[rows, :] = x + emb * gate


def _ple(x, p, proj, gate, *, layer, name):
    t, d = x.shape
    row = pl.BlockSpec((ROW_TM, d), lambda i: (i, 0))
    return pl.pallas_call(
        _ple_kernel,
        grid=(t // ROW_TM,),
        in_specs=[row, pl.BlockSpec((None, ROW_TM, PLE_DIM), lambda i: (layer, i, 0)),
                  pl.BlockSpec((None, PLE_DIM, d), lambda i: (layer, 0, 0)),
                  pl.BlockSpec((None, d, d), lambda i: (layer, 0, 0))],
        out_specs=row,
        out_shape=jax.ShapeDtypeStruct((t, d), F32),
        compiler_params=_params("arbitrary"),
        name=name,
    )(x, p, proj, gate)


ROPE_TM = 512


def _rope_table_kernel(pos_ref, freq_ref, cos_ref, sin_ref):
    ang = pos_ref[...] * freq_ref[...]
    lane = lax.broadcasted_iota(jnp.int32, ang.shape, 1)
    cos_ref[...] = jnp.cos(ang)
    sin_ref[...] = jnp.where(lane < HEAD_DIM // 2, -1.0, 1.0) * jnp.sin(ang)


def _rope_tables(positions):
    t = positions.size
    half = HEAD_DIM // 2
    inv_freq = ROPE_THETA ** (-jnp.arange(half, dtype=F32) * (2.0 / HEAD_DIM))
    freq = jnp.concatenate([inv_freq, inv_freq]).reshape(1, HEAD_DIM)
    pos = positions.astype(F32).reshape(t, 1)
    tab = pl.BlockSpec((ROPE_TM, HEAD_DIM), lambda i: (i, 0))
    return pl.pallas_call(
        _rope_table_kernel,
        grid=(t // ROPE_TM,),
        in_specs=[pl.BlockSpec((ROPE_TM, 1), lambda i: (i, 0)), pl.BlockSpec((1, HEAD_DIM), lambda i: (0, 0))],
        out_specs=[tab, tab],
        out_shape=[jax.ShapeDtypeStruct((t, HEAD_DIM), F32)] * 2,
        compiler_params=_params("arbitrary"),
        name="rope_tables",
    )(pos, freq)


def _rotary(v, cos, sin_signed):
    return v * cos + pltpu.roll(v, HEAD_DIM // 2, axis=1) * sin_signed


HEADS_TM = 1024
HEADS_TN = 1024
HEADS_PER_TILE = HEADS_TN // HEAD_DIM
HEADS_CHUNK = 256
HEADS_SUB_M = 256


def _heads_kernel(x_ref, w_ref, cos_ref, sin_ref, g_ref, b_ref, *refs, dilations, ln_prologue, n_rotary_tiles):
    assert set(dilations) <= {1, 4, 16}
    outs = dict(zip(dilations, refs[:len(dilations)]))
    xb_ref, stage_ref, stage2_ref = refs[len(dilations):]
    quarter = HEADS_SUB_M // 4
    j = pl.program_id(2)

    @pl.when(j == 0)
    def _():
        xv = x_ref[0]
        if ln_prologue:
            xv = _layer_norm(xv, g_ref[...], b_ref[...])
        xb_ref[...] = xv.astype(BF16)

    cos, sin = cos_ref[...], sin_ref[...]
    if n_rotary_tiles is not None:
        rotate = j < n_rotary_tiles
        cos = jnp.where(rotate, cos, 1.0)
        sin = jnp.where(rotate, sin, 0.0)

    for c in range(HEADS_TN // HEADS_CHUNK):
        for m0 in range(0, HEADS_TM, HEADS_SUB_M):
            res = _mm(xb_ref[pl.ds(m0, HEADS_SUB_M), :], w_ref[:, c * HEADS_CHUNK:(c + 1) * HEADS_CHUNK])
            for ch in range(HEADS_CHUNK // HEAD_DIM):
                hh = c * (HEADS_CHUNK // HEAD_DIM) + ch
                v = _rotary(res[:, ch * HEAD_DIM:(ch + 1) * HEAD_DIM],
                            cos[m0:m0 + HEADS_SUB_M], sin[m0:m0 + HEADS_SUB_M])
                if 1 in outs:
                    outs[1][0, hh, pl.ds(m0, HEADS_SUB_M), :] = v.astype(BF16)
                if max(dilations) == 1:
                    continue
                stage_ref[hh, pl.ds(m0, HEADS_SUB_M), :] = v
                for r4 in range(4):
                    part = stage_ref[hh, pl.ds(m0 + r4, quarter, stride=4), :]
                    if 4 in outs:
                        outs[4][0, hh, pl.ds(m0 // 4, quarter), r4 * HEAD_DIM:(r4 + 1) * HEAD_DIM] = part.astype(BF16)
                    if 16 not in outs:
                        continue
                    stage2_ref[hh, pl.ds(m0 + r4 * quarter, quarter), :] = part
                    for q4 in range(4):
                        r16 = 4 * q4 + r4
                        sub = stage2_ref[hh, pl.ds(m0 + r4 * quarter + q4, quarter // 4, stride=4), :]
                        outs[16][0, hh, pl.ds(m0 // 16, quarter // 4), r16 * HEAD_DIM:(r16 + 1) * HEAD_DIM] = (
                            sub.astype(BF16))


def _heads_proj(x, w, cos, sin, g, b, *, n_out, col_tile_offset, dilations, ln_prologue, n_rotary_tiles, name):
    b_, s, d = x.shape
    per_b = s // HEADS_TM
    tab = pl.BlockSpec((HEADS_TM, HEAD_DIM), lambda bb, i, j: (bb * per_b + i, 0))
    vec = pl.BlockSpec((1, d), lambda bb, i, j: (0, 0))
    n_heads = n_out // HEAD_DIM
    return pl.pallas_call(
        functools.partial(_heads_kernel, dilations=dilations, ln_prologue=ln_prologue, n_rotary_tiles=n_rotary_tiles),
        grid=(b_, per_b, n_out // HEADS_TN),
        in_specs=[pl.BlockSpec((1, HEADS_TM, d), lambda bb, i, j: (bb, i, 0)),
                  pl.BlockSpec((d, HEADS_TN), lambda bb, i, j: (0, col_tile_offset + j)), tab, tab, vec, vec],
        out_specs=[pl.BlockSpec((1, HEADS_PER_TILE, HEADS_TM // dil, dil * HEAD_DIM), lambda bb, i, j: (bb, j, i, 0))
                   for dil in dilations],
        out_shape=[jax.ShapeDtypeStruct((b_, n_heads, s // dil, dil * HEAD_DIM), BF16) for dil in dilations],
        scratch_shapes=[pltpu.VMEM((HEADS_TM, d), BF16)]
        + [pltpu.VMEM((HEADS_PER_TILE, HEADS_TM, HEAD_DIM), F32)] * 2,
        compiler_params=_params("arbitrary", "arbitrary", "arbitrary"),
        name=name,
    )(x, w, cos, sin, g, b)


MERGE_ROWS = 256
LOG2_E = 1.4426950408889634


def _attn_kernel(*refs):
    group_refs = [refs[5 * g:5 * g + 5] for g in range(N_GROUPS)]
    o_ref, og_ref, lse_ref, bias_ref = refs[5 * N_GROUPS:]
    tile = pl.program_id(2)
    scale = HEAD_DIM ** -0.5
    nt_dims = (((1,), (1,)), ((), ()))

    qi = lax.broadcasted_iota(jnp.int32, (ATTN_BLOCK, 2 * ATTN_BLOCK), 0)
    kj = lax.broadcasted_iota(jnp.int32, (ATTN_BLOCK, 2 * ATTN_BLOCK), 1)
    bias = jnp.where(jnp.logical_and(kj >= qi, kj <= qi + ATTN_BLOCK), 0.0, -jnp.inf)
    bias_ref[0] = bias
    bias_ref[1] = jnp.where(jnp.logical_or(tile > 0, kj >= ATTN_BLOCK), bias, -jnp.inf)

    for g, (dil, (q_ref, kc_ref, kp_ref, vc_ref, vp_ref)) in enumerate(zip(DILATIONS, group_refs)):
        n_blocks = ATTN_TILE // (ATTN_BLOCK * dil)
        for r in range(dil):
            lanes = slice(r * HEAD_DIM, (r + 1) * HEAD_DIM)
            k_prev = kp_ref[0, 0, :, lanes]
            v_prev = vp_ref[0, 0, :, lanes]
            for n in range(n_blocks):
                rows = slice(n * ATTN_BLOCK, (n + 1) * ATTN_BLOCK)
                q = q_ref[0, 0, rows, lanes]
                k_cur = kc_ref[0, 0, rows, lanes]
                v_cur = vc_ref[0, 0, rows, lanes]
                keys = jnp.concatenate([k_prev, k_cur], axis=0)
                vals = jnp.concatenate([v_prev, v_cur], axis=0)
                s = lax.dot_general(q, keys, nt_dims, preferred_element_type=F32) + bias_ref[1 if n == 0 else 0]
                m = jnp.max(s, axis=1, keepdims=True)
                p = jnp.exp2((s - m) * (scale * LOG2_E))
                l = jnp.sum(p, axis=1, keepdims=True)
                pv = _mm(p.astype(BF16), vals)
                if dil == 1:
                    dst = pl.ds(n * ATTN_BLOCK, ATTN_BLOCK)
                else:
                    dst = pl.ds(n * ATTN_BLOCK * dil + r, ATTN_BLOCK, stride=dil)
                og_ref[g, dst, :] = pv / l
                lse_ref[g, dst, :] = jnp.broadcast_to(m * scale + jnp.log(l), (ATTN_BLOCK, HEAD_DIM))
                k_prev, v_prev = k_cur, v_cur

    for c in range(ATTN_TILE // MERGE_ROWS):
        rows = slice(c * MERGE_ROWS, (c + 1) * MERGE_ROWS)
        lses = [lse_ref[g, rows, :] for g in range(N_GROUPS)]
        top = functools.reduce(jnp.maximum, lses)
        wts = [jnp.exp(v - top) for v in lses]
        num = sum(w * og_ref[g, rows, :] for g, w in enumerate(wts))
        o_ref[0, rows, :] = (num / sum(wts)).astype(o_ref.dtype)


def _attention(q_views, kv_views):
    hd = HEAD_DIM
    b_ = q_views[0].shape[0]
    s = q_views[0].shape[2]
    n_tiles = s // ATTN_TILE
    operands, in_specs = [], []
    for dil, qv, kvv in zip(DILATIONS, q_views, kv_views):
        rows, width = ATTN_TILE // dil, dil * hd
        prev_per_tile = rows // ATTN_BLOCK

        def cur(head_offset):
            return pl.BlockSpec((1, 1, rows, width), lambda b, h, t, o=head_offset: (b, o + h, t, 0))

        def prev(head_offset):
            return pl.BlockSpec((1, 1, ATTN_BLOCK, width),
                                lambda b, h, t, o=head_offset, n=prev_per_tile: (b, o + h, jnp.maximum(n * t - 1, 0), 0))

        operands += [qv, kvv, kvv, kvv, kvv]
        in_specs += [cur(0), cur(0), prev(0), cur(N_HEADS), prev(N_HEADS)]

    return pl.pallas_call(
        _attn_kernel,
        grid=(b_, N_HEADS, n_tiles),
        in_specs=in_specs,
        out_specs=pl.BlockSpec((1, ATTN_TILE, hd), lambda b, h, t: (b, t, h)),
        out_shape=jax.ShapeDtypeStruct((b_, s, N_HEADS * hd), BF16),
        scratch_shapes=[pltpu.VMEM((N_GROUPS, ATTN_TILE, hd), F32), pltpu.VMEM((N_GROUPS, ATTN_TILE, hd), F32),
                        pltpu.VMEM((2, ATTN_BLOCK, 2 * ATTN_BLOCK), F32)],
        compiler_params=_params("arbitrary", "arbitrary", "arbitrary"),
        name="dilated_attention",
    )(*operands)


def kernel(x, p, positions, conv_w_in, conv_b_in, conv_dw, conv_dw_b, conv_ln_g, conv_ln_b, conv_w_out,
           kv_ln_g, kv_ln_b, w_kv, attn_w_q, attn_w_o, ln1_g, ln1_b, mlp_up, mlp_down, ln2_g, ln2_b,
           ple_proj, ple_gate):
    b_, s, d = x.shape
    t = b_ * s
    assert s % ATTN_TILE == 0 and d == D_MODEL

    def vec(v):
        return v.reshape(1, -1)

    def wb(w):
        return w.astype(BF16)

    cos, sin = _rope_tables(positions)

    pf = p.reshape(DEPTH, t, PLE_DIM)
    conv, (up, down, w_out) = _glu_conv(x, wb(conv_w_in[0]), vec(conv_b_in[0]), conv_dw[0],
                                        conv_dw_b[0].reshape(1, 1, d), [mlp_up, mlp_down, conv_w_out[0]])
    xf = x.reshape(t, d)
    xf = _proj_ln(conv.reshape(t, 1, d), xf, w_out, vec(conv_ln_g[0]), vec(conv_ln_b[0]),
                  vec(ln1_g[0]), vec(ln1_b[0]), conv_prologue=True, name="conv_out_ln1")
    xf, (gate, w_kv_b, w_q, w_o) = _mlp(xf, up, down, vec(ln2_g[0]), vec(ln2_b[0]), layer=0, name="mlp0",
                                        riders=[ple_gate, w_kv, attn_w_q[0], attn_w_o[0]])
    proj = wb(ple_proj)
    xf = _ple(xf, pf, proj, gate, layer=0, name="ple0")

    x3 = xf.reshape(b_, s, d)
    kv_views = _heads_proj(x3, w_kv_b, cos, sin, vec(kv_ln_g), vec(kv_ln_b), n_out=2 * d, col_tile_offset=0,
                           dilations=DILATIONS, ln_prologue=True, n_rotary_tiles=d // HEADS_TN, name="kv_proj")
    q_views = [
        _heads_proj(x3, w_q, cos, sin, vec(kv_ln_g), vec(kv_ln_b), n_out=d, col_tile_offset=g * (d // HEADS_TN),
                    dilations=(dil,), ln_prologue=False, n_rotary_tiles=None, name=f"q_proj_d{dil}")[0]
        for g, dil in enumerate(DILATIONS)]
    o = _attention(q_views, kv_views)
    xf = _proj_ln(o.reshape(t, d), xf, w_o, vec(ln1_g[1]), vec(ln1_b[1]), vec(ln1_g[1]), vec(ln1_b[1]),
                  conv_prologue=False, name="attn_out_ln1")
    xf, _ = _mlp(xf, up, down, vec(ln2_g[1]), vec(ln2_b[1]), layer=1, name="mlp1")
    xf = _ple(xf, pf, proj, gate, layer=1, name="ple1")
    return xf.reshape(b_, s, d)
```

```python
import functools

import jax
import jax.numpy as jnp
from jax import lax
from jax.experimental import pallas as pl
from jax.experimental.pallas import tpu as pltpu

D_MODEL = 2048
DEPTH = 2
HEAD_DIM = 128
N_HEADS = D_MODEL // HEAD_DIM
DILATIONS = (1, 4, 16)
N_GROUPS = len(DILATIONS)
ATTN_BLOCK = 128
ATTN_TILE = ATTN_BLOCK * DILATIONS[-1]
CONV_WIDTH = 31
CONV_HALO = 32
D_FF = 4 * D_MODEL
PLE_DIM = 256
ROPE_THETA = 10000.0
LN_EPS = 1e-5
DEEPNORM_ALPHA = (2 * DEPTH) ** 0.25

VMEM_LIMIT_BYTES = 58 * 1024 * 1024
BF16_SUBLANES = 16
BF16 = jnp.bfloat16
F32 = jnp.float32


def _params(*semantics):
    return pltpu.CompilerParams(dimension_semantics=semantics, vmem_limit_bytes=VMEM_LIMIT_BYTES)


def _layer_norm(v, g, b):
    mu = jnp.mean(v, axis=-1, keepdims=True)
    c = v - mu
    var = jnp.mean(c * c, axis=-1, keepdims=True)
    return c * lax.rsqrt(var + LN_EPS) * g + b


def _mm(a, b):
    return jnp.dot(a, b, preferred_element_type=F32)


def _split(refs, *sizes):
    out, at = [], 0
    for n in sizes:
        out.append(refs[at:at + n])
        at += n
    assert at == len(refs)
    return out


class _Riders:
    def __init__(self, arrays, steps, step_of_grid):
        self.shapes = [a.shape for a in arrays]
        self.operands = [a.reshape(-1, a.shape[-1]) for a in arrays]
        self.specs, self.out_shapes = [], []
        for a in self.operands:
            rows, width = a.shape[0] // steps, a.shape[1]
            assert rows * steps == a.shape[0] and rows % BF16_SUBLANES == 0
            self.specs.append(pl.BlockSpec((rows, width), lambda *g: (step_of_grid(*g), 0)))
            self.out_shapes.append(jax.ShapeDtypeStruct(a.shape, BF16))

    def restore(self, rounded):
        return [r.reshape(s) for r, s in zip(rounded, self.shapes)]


def _round_riders(in_refs, out_refs):
    for src, dst in zip(in_refs, out_refs):
        dst[...] = src[...].astype(BF16)


CONV_TM = 256
CONV_TN = 1024
CONV_ROWS = 16


def _glu_conv_kernel(x_ref, wa_ref, wg_ref, ba_ref, bg_ref, dw_ref, dwb_ref, *refs, n_riders):
    rider_in, (o_ref,), rider_out, (ubuf,) = _split(refs, n_riders, 1, n_riders, 1)
    _round_riders(rider_in, rider_out)
    i = pl.program_id(2)

    @pl.when(i == 0)
    def _():
        ubuf[pl.ds(0, CONV_HALO)] = jnp.zeros((CONV_HALO, 1, CONV_TN), F32)

    xb = x_ref[0].astype(BF16)
    a = _mm(xb, wa_ref[...]) + ba_ref[...]
    g = _mm(xb, wg_ref[...]) + bg_ref[...]
    ubuf[pl.ds(CONV_HALO, CONV_TM), 0, :] = a * jax.nn.sigmoid(g)

    taps = [dw_ref[k, 0] for k in range(CONV_WIDTH)]
    bias = dwb_ref[0, 0]
    first = CONV_HALO - (CONV_WIDTH - 1)

    def body(c, carry):
        t0 = c * CONV_ROWS
        acc = [bias] * CONV_ROWS
        for e in range(CONV_ROWS + CONV_WIDTH - 1):
            row = ubuf[t0 + first + e, 0]
            for r in range(max(0, e - CONV_WIDTH + 1), min(CONV_ROWS, e + 1)):
                acc[r] = acc[r] + taps[e - r] * row
        for r in range(CONV_ROWS):
            o_ref[0, t0 + r, 0] = acc[r]
        return carry

    lax.fori_loop(0, CONV_TM // CONV_ROWS, body, 0)
    ubuf[pl.ds(0, CONV_HALO)] = ubuf[pl.ds(CONV_TM, CONV_HALO)]


def _glu_conv(x, w_in, b_in, dw, dw_b, riders):
    b_, s, d = x.shape
    nj = d // CONV_TN
    ni = s // CONV_TM
    rider = _Riders(riders, nj * b_ * ni, lambda j, b, i: (j * b_ + b) * ni + i)
    conv, *rounded = pl.pallas_call(
        functools.partial(_glu_conv_kernel, n_riders=len(riders)),
        grid=(nj, b_, ni),
        in_specs=[
            pl.BlockSpec((1, CONV_TM, d), lambda j, b, i: (b, i, 0)),
            pl.BlockSpec((d, CONV_TN), lambda j, b, i: (0, j)),
            pl.BlockSpec((d, CONV_TN), lambda j, b, i: (0, j + nj)),
            pl.BlockSpec((1, CONV_TN), lambda j, b, i: (0, j)),
            pl.BlockSpec((1, CONV_TN), lambda j, b, i: (0, j + nj)),
            pl.BlockSpec((CONV_WIDTH, 1, CONV_TN), lambda j, b, i: (0, 0, j)),
            pl.BlockSpec((1, 1, CONV_TN), lambda j, b, i: (0, 0, j)),
            *rider.specs,
        ],
        out_specs=[pl.BlockSpec((1, CONV_TM, 1, CONV_TN), lambda j, b, i: (b, i, 0, j)), *rider.specs],
        out_shape=[jax.ShapeDtypeStruct((b_, s, 1, d), F32), *rider.out_shapes],
        scratch_shapes=[pltpu.VMEM((CONV_HALO + CONV_TM, 1, CONV_TN), F32)],
        compiler_params=_params("arbitrary", "arbitrary", "arbitrary"),
        name="glu_conv",
    )(x, w_in, w_in, b_in, b_in, dw, dw_b, *rider.operands)
    return conv, rider.restore(rounded)


ROW_TM = 512
ROW_SPLIT = 2


def _proj_ln_kernel(act_ref, x_ref, w_ref, pg_ref, pb_ref, g_ref, b_ref, o_ref, *scratch, conv_prologue):
    for h in range(ROW_SPLIT):
        rows = pl.ds(h * (ROW_TM // ROW_SPLIT), ROW_TM // ROW_SPLIT)
        if conv_prologue:
            a2d_ref, = scratch
            a2d_ref[rows, :] = act_ref[rows, 0, :]
            a = _layer_norm(a2d_ref[rows, :], pg_ref[...], pb_ref[...])
            a = a * jax.nn.sigmoid(a)
        else:
            a = act_ref[rows, :]
        mix = _mm(a.astype(BF16), w_ref[...])
        o_ref[rows, :] = _layer_norm(DEEPNORM_ALPHA * x_ref[rows, :] + mix, g_ref[...], b_ref[...])


def _proj_ln(act, x, w, pro_g, pro_b, g, b, *, conv_prologue, name):
    t, d = x.shape
    row = pl.BlockSpec((ROW_TM, d), lambda i: (i, 0))
    vec = pl.BlockSpec((1, d), lambda i: (0, 0))
    act_spec = pl.BlockSpec((ROW_TM, 1, d), lambda i: (i, 0, 0)) if conv_prologue else row
    return pl.pallas_call(
        functools.partial(_proj_ln_kernel, conv_prologue=conv_prologue),
        grid=(t // ROW_TM,),
        in_specs=[act_spec, row, pl.BlockSpec((d, d), lambda i: (0, 0)), vec, vec, vec, vec],
        out_specs=row,
        out_shape=jax.ShapeDtypeStruct((t, d), F32),
        scratch_shapes=[pltpu.VMEM((ROW_TM, d), F32)] if conv_prologue else [],
        compiler_params=_params("arbitrary"),
        name=name,
    )(act, x, w, pro_g, pro_b, g, b)


MLP_TM = 1024
MLP_TF = 512
MLP_LAST_ROWS = 256


def _mlp_kernel(x_ref, up_ref, down_ref, g_ref, b_ref, *refs, n_riders):
    rider_in, (o_ref,), rider_out, (xb_ref,) = _split(refs, n_riders, 1, n_riders, 1)
    _round_riders(rider_in, rider_out)
    k = pl.program_id(1)

    @pl.when(k == 0)
    def _():
        xb_ref[...] = x_ref[...].astype(BF16)
        o_ref[...] = jnp.zeros_like(o_ref)

    def partial_out(rows):
        h = jnp.maximum(_mm(xb_ref[rows, :], up_ref[...]), 0.0)
        return _mm((h * h).astype(BF16), down_ref[...])

    last = pl.num_programs(1) - 1

    @pl.when(k < last)
    def _():
        o_ref[...] += partial_out(slice(None))

    @pl.when(k == last)
    def _():
        for r0 in range(0, MLP_TM, MLP_LAST_ROWS):
            rows = pl.ds(r0, MLP_LAST_ROWS)
            y = o_ref[rows, :] + partial_out(rows)
            o_ref[rows, :] = _layer_norm(DEEPNORM_ALPHA * x_ref[rows, :] + y, g_ref[...], b_ref[...])


def _mlp(x, up, down, g, b, *, layer, name, riders=()):
    t, d = x.shape
    ff = up.shape[2]
    nk = ff // MLP_TF
    row = pl.BlockSpec((MLP_TM, d), lambda i, k: (i, 0))
    vec = pl.BlockSpec((1, d), lambda i, k: (0, 0))
    rider = _Riders(riders, (t // MLP_TM) * nk, lambda i, k: i * nk + k)
    out, *rounded = pl.pallas_call(
        functools.partial(_mlp_kernel, n_riders=len(riders)),
        grid=(t // MLP_TM, nk),
        in_specs=[row, pl.BlockSpec((None, d, MLP_TF), lambda i, k: (layer, 0, k)),
                  pl.BlockSpec((None, MLP_TF, d), lambda i, k: (layer, k, 0)), vec, vec, *rider.specs],
        out_specs=[row, *rider.specs],
        out_shape=[jax.ShapeDtypeStruct((t, d), F32), *rider.out_shapes],
        scratch_shapes=[pltpu.VMEM((MLP_TM, d), BF16)],
        compiler_params=_params("arbitrary", "arbitrary"),
        name=name,
    )(x, up, down, g, b, *rider.operands)
    return out, rider.restore(rounded)


def _ple_kernel(x_ref, p_ref, proj_ref, gate_ref, o_ref):
    for h in range(ROW_SPLIT):
        rows = pl.ds(h * (ROW_TM // ROW_SPLIT), ROW_TM // ROW_SPLIT)
        x = x_ref[rows, :]
        emb = _mm(p_ref[rows, :].astype(BF16), proj_ref[...])
        gate = jax.nn.sigmoid(_mm(x.astype(BF16), gate_ref[...]))
        o_ref[rows, :] = x + emb * gate


def _ple(x, p, proj, gate, *, layer, name):
    t, d = x.shape
    row = pl.BlockSpec((ROW_TM, d), lambda i: (i, 0))
    return pl.pallas_call(
        _ple_kernel,
        grid=(t // ROW_TM,),
        in_specs=[row, pl.BlockSpec((None, ROW_TM, PLE_DIM), lambda i: (layer, i, 0)),
                  pl.BlockSpec((None, PLE_DIM, d), lambda i: (layer, 0, 0)),
                  pl.BlockSpec((None, d, d), lambda i: (layer, 0, 0))],
        out_specs=row,
        out_shape=jax.ShapeDtypeStruct((t, d), F32),
        compiler_params=_params("arbitrary"),
        name=name,
    )(x, p, proj, gate)


ROPE_TM = 512


def _rope_table_kernel(pos_ref, freq_ref, cos_ref, sin_ref):
    ang = pos_ref[...] * freq_ref[...]
    lane = lax.broadcasted_iota(jnp.int32, ang.shape, 1)
    cos_ref[...] = jnp.cos(ang)
    sin_ref[...] = jnp.where(lane < HEAD_DIM // 2, -1.0, 1.0) * jnp.sin(ang)


def _rope_tables(positions):
    t = positions.size
    half = HEAD_DIM // 2
    inv_freq = ROPE_THETA ** (-jnp.arange(half, dtype=F32) * (2.0 / HEAD_DIM))
    freq = jnp.concatenate([inv_freq, inv_freq]).reshape(1, HEAD_DIM)
    pos = positions.astype(F32).reshape(t, 1)
    tab = pl.BlockSpec((ROPE_TM, HEAD_DIM), lambda i: (i, 0))
    return pl.pallas_call(
        _rope_table_kernel,
        grid=(t // ROPE_TM,),
        in_specs=[pl.BlockSpec((ROPE_TM, 1), lambda i: (i, 0)), pl.BlockSpec((1, HEAD_DIM), lambda i: (0, 0))],
        out_specs=[tab, tab],
        out_shape=[jax.ShapeDtypeStruct((t, HEAD_DIM), F32)] * 2,
        compiler_params=_params("arbitrary"),
        name="rope_tables",
    )(pos, freq)


def _rotary(v, cos, sin_signed):
    return v * cos + pltpu.roll(v, HEAD_DIM // 2, axis=1) * sin_signed


HEADS_TM = 1024
HEADS_TN = 1024
HEADS_PER_TILE = HEADS_TN // HEAD_DIM
HEADS_CHUNK = 256
HEADS_SUB_M = 256


def _heads_kernel(x_ref, w_ref, cos_ref, sin_ref, g_ref, b_ref, *refs, dilations, ln_prologue, n_rotary_tiles):
    assert set(dilations) <= {1, 4, 16}
    outs = dict(zip(dilations, refs[:len(dilations)]))
    xb_ref, stage_ref, stage2_ref = refs[len(dilations):]
    quarter = HEADS_SUB_M // 4
    j = pl.program_id(2)

    @pl.when(j == 0)
    def _():
        xv = x_ref[0]
        if ln_prologue:
            xv = _layer_norm(xv, g_ref[...], b_ref[...])
        xb_ref[...] = xv.astype(BF16)

    cos, sin = cos_ref[...], sin_ref[...]
    if n_rotary_tiles is not None:
        rotate = j < n_rotary_tiles
        cos = jnp.where(rotate, cos, 1.0)
        sin = jnp.where(rotate, sin, 0.0)

    for c in range(HEADS_TN // HEADS_CHUNK):
        for m0 in range(0, HEADS_TM, HEADS_SUB_M):
            res = _mm(xb_ref[pl.ds(m0, HEADS_SUB_M), :], w_ref[:, c * HEADS_CHUNK:(c + 1) * HEADS_CHUNK])
            for ch in range(HEADS_CHUNK // HEAD_DIM):
                hh = c * (HEADS_CHUNK // HEAD_DIM) + ch
                v = _rotary(res[:, ch * HEAD_DIM:(ch + 1) * HEAD_DIM],
                            cos[m0:m0 + HEADS_SUB_M], sin[m0:m0 + HEADS_SUB_M])
                if 1 in outs:
                    outs[1][0, hh, pl.ds(m0, HEADS_SUB_M), :] = v.astype(BF16)
                if max(dilations) == 1:
                    continue
                stage_ref[hh, pl.ds(m0, HEADS_SUB_M), :] = v
                for r4 in range(4):
                    part = stage_ref[hh, pl.ds(m0 + r4, quarter, stride=4), :]
                    if 4 in outs:
                        outs[4][0, hh, pl.ds(m0 // 4, quarter), r4 * HEAD_DIM:(r4 + 1) * HEAD_DIM] = part.astype(BF16)
                    if 16 not in outs:
                        continue
                    stage2_ref[hh, pl.ds(m0 + r4 * quarter, quarter), :] = part
                    for q4 in range(4):
                        r16 = 4 * q4 + r4
                        sub = stage2_ref[hh, pl.ds(m0 + r4 * quarter + q4, quarter // 4, stride=4), :]
                        outs[16][0, hh, pl.ds(m0 // 16, quarter // 4), r16 * HEAD_DIM:(r16 + 1) * HEAD_DIM] = (
                            sub.astype(BF16))


def _heads_proj(x, w, cos, sin, g, b, *, n_out, col_tile_offset, dilations, ln_prologue, n_rotary_tiles, name):
    b_, s, d = x.shape
    per_b = s // HEADS_TM
    tab = pl.BlockSpec((HEADS_TM, HEAD_DIM), lambda bb, i, j: (bb * per_b + i, 0))
    vec = pl.BlockSpec((1, d), lambda bb, i, j: (0, 0))
    n_heads = n_out // HEAD_DIM
    return pl.pallas_call(
        functools.partial(_heads_kernel, dilations=dilations, ln_prologue=ln_prologue, n_rotary_tiles=n_rotary_tiles),
        grid=(b_, per_b, n_out // HEADS_TN),
        in_specs=[pl.BlockSpec((1, HEADS_TM, d), lambda bb, i, j: (bb, i, 0)),
                  pl.BlockSpec((d, HEADS_TN), lambda bb, i, j: (0, col_tile_offset + j)), tab, tab, vec, vec],
        out_specs=[pl.BlockSpec((1, HEADS_PER_TILE, HEADS_TM // dil, dil * HEAD_DIM), lambda bb, i, j: (bb, j, i, 0))
                   for dil in dilations],
        out_shape=[jax.ShapeDtypeStruct((b_, n_heads, s // dil, dil * HEAD_DIM), BF16) for dil in dilations],
        scratch_shapes=[pltpu.VMEM((HEADS_TM, d), BF16)]
        + [pltpu.VMEM((HEADS_PER_TILE, HEADS_TM, HEAD_DIM), F32)] * 2,
        compiler_params=_params("arbitrary", "arbitrary", "arbitrary"),
        name=name,
    )(x, w, cos, sin, g, b)


MERGE_ROWS = 256
LOG2_E = 1.4426950408889634


def _attn_kernel(*refs):
    group_refs = [refs[5 * g:5 * g + 5] for g in range(N_GROUPS)]
    o_ref, og_ref, lse_ref, bias_ref = refs[5 * N_GROUPS:]
    tile = pl.program_id(2)
    scale = HEAD_DIM ** -0.5
    nt_dims = (((1,), (1,)), ((), ()))

    qi = lax.broadcasted_iota(jnp.int32, (ATTN_BLOCK, 2 * ATTN_BLOCK), 0)
    kj = lax.broadcasted_iota(jnp.int32, (ATTN_BLOCK, 2 * ATTN_BLOCK), 1)
    bias = jnp.where(jnp.logical_and(kj >= qi, kj <= qi + ATTN_BLOCK), 0.0, -jnp.inf)
    bias_ref[0] = bias
    bias_ref[1] = jnp.where(jnp.logical_or(tile > 0, kj >= ATTN_BLOCK), bias, -jnp.inf)

    for g, (dil, (q_ref, kc_ref, kp_ref, vc_ref, vp_ref)) in enumerate(zip(DILATIONS, group_refs)):
        n_blocks = ATTN_TILE // (ATTN_BLOCK * dil)
        for r in range(dil):
            lanes = slice(r * HEAD_DIM, (r + 1) * HEAD_DIM)
            k_prev = kp_ref[0, 0, :, lanes]
            v_prev = vp_ref[0, 0, :, lanes]
            for n in range(n_blocks):
                rows = slice(n * ATTN_BLOCK, (n + 1) * ATTN_BLOCK)
                q = q_ref[0, 0, rows, lanes]
                k_cur = kc_ref[0, 0, rows, lanes]
                v_cur = vc_ref[0, 0, rows, lanes]
                keys = jnp.concatenate([k_prev, k_cur], axis=0)
                vals = jnp.concatenate([v_prev, v_cur], axis=0)
                s = lax.dot_general(q, keys, nt_dims, preferred_element_type=F32) + bias_ref[1 if n == 0 else 0]
                m = jnp.max(s, axis=1, keepdims=True)
                p = jnp.exp2((s - m) * (scale * LOG2_E))
                l = jnp.sum(p, axis=1, keepdims=True)
                pv = _mm(p.astype(BF16), vals)
                if dil == 1:
                    dst = pl.ds(n * ATTN_BLOCK, ATTN_BLOCK)
                else:
                    dst = pl.ds(n * ATTN_BLOCK * dil + r, ATTN_BLOCK, stride=dil)
                og_ref[g, dst, :] = pv / l
                lse_ref[g, dst, :] = jnp.broadcast_to(m * scale + jnp.log(l), (ATTN_BLOCK, HEAD_DIM))
                k_prev, v_prev = k_cur, v_cur

    for c in range(ATTN_TILE // MERGE_ROWS):
        rows = slice(c * MERGE_ROWS, (c + 1) * MERGE_ROWS)
        lses = [lse_ref[g, rows, :] for g in range(N_GROUPS)]
        top = functools.reduce(jnp.maximum, lses)
        wts = [jnp.exp(v - top) for v in lses]
        num = sum(w * og_ref[g, rows, :] for g, w in enumerate(wts))
        o_ref[0, rows, :] = (num / sum(wts)).astype(o_ref.dtype)


def _attention(q_views, kv_views):
    hd = HEAD_DIM
    b_ = q_views[0].shape[0]
    s = q_views[0].shape[2]
    n_tiles = s // ATTN_TILE
    operands, in_specs = [], []
    for dil, qv, kvv in zip(DILATIONS, q_views, kv_views):
        rows, width = ATTN_TILE // dil, dil * hd
        prev_per_tile = rows // ATTN_BLOCK

        def cur(head_offset):
            return pl.BlockSpec((1, 1, rows, width), lambda b, h, t, o=head_offset: (b, o + h, t, 0))

        def prev(head_offset):
            return pl.BlockSpec((1, 1, ATTN_BLOCK, width),
                                lambda b, h, t, o=head_offset, n=prev_per_tile: (b, o + h, jnp.maximum(n * t - 1, 0), 0))

        operands += [qv, kvv, kvv, kvv, kvv]
        in_specs += [cur(0), cur(0), prev(0), cur(N_HEADS), prev(N_HEADS)]

    return pl.pallas_call(
        _attn_kernel,
        grid=(b_, N_HEADS, n_tiles),
        in_specs=in_specs,
        out_specs=pl.BlockSpec((1, ATTN_TILE, hd), lambda b, h, t: (b, t, h)),
        out_shape=jax.ShapeDtypeStruct((b_, s, N_HEADS * hd), BF16),
        scratch_shapes=[pltpu.VMEM((N_GROUPS, ATTN_TILE, hd), F32), pltpu.VMEM((N_GROUPS, ATTN_TILE, hd), F32),
                        pltpu.VMEM((2, ATTN_BLOCK, 2 * ATTN_BLOCK), F32)],
        compiler_params=_params("arbitrary", "arbitrary", "arbitrary"),
        name="dilated_attention",
    )(*operands)


def kernel(x, p, positions, conv_w_in, conv_b_in, conv_dw, conv_dw_b, conv_ln_g, conv_ln_b, conv_w_out,
           kv_ln_g, kv_ln_b, w_kv, attn_w_q, attn_w_o, ln1_g, ln1_b, mlp_up, mlp_down, ln2_g, ln2_b,
           ple_proj, ple_gate):
    b_, s, d = x.shape
    t = b_ * s
    assert s % ATTN_TILE == 0 and d == D_MODEL

    def vec(v):
        return v.reshape(1, -1)

    def wb(w):
        return w.astype(BF16)

    cos, sin = _rope_tables(positions)

    pf = p.reshape(DEPTH, t, PLE_DIM)
    conv, (up, down, w_out) = _glu_conv(x, wb(conv_w_in[0]), vec(conv_b_in[0]), conv_dw[0],
                                        conv_dw_b[0].reshape(1, 1, d), [mlp_up, mlp_down, conv_w_out[0]])
    xf = x.reshape(t, d)
    xf = _proj_ln(conv.reshape(t, 1, d), xf, w_out, vec(conv_ln_g[0]), vec(conv_ln_b[0]),
                  vec(ln1_g[0]), vec(ln1_b[0]), conv_prologue=True, name="conv_out_ln1")
    xf, (gate, w_kv_b, w_q, w_o) = _mlp(xf, up, down, vec(ln2_g[0]), vec(ln2_b[0]), layer=0, name="mlp0",
                                        riders=[ple_gate, w_kv, attn_w_q[0], attn_w_o[0]])
    proj = wb(ple_proj)
    xf = _ple(xf, pf, proj, gate, layer=0, name="ple0")

    x3 = xf.reshape(b_, s, d)
    kv_views = _heads_proj(x3, w_kv_b, cos, sin, vec(kv_ln_g), vec(kv_ln_b), n_out=2 * d, col_tile_offset=0,
                           dilations=DILATIONS, ln_prologue=True, n_rotary_tiles=d // HEADS_TN, name="kv_proj")
    q_views = [
        _heads_proj(x3, w_q, cos, sin, vec(kv_ln_g), vec(kv_ln_b), n_out=d, col_tile_offset=g * (d // HEADS_TN),
                    dilations=(dil,), ln_prologue=False, n_rotary_tiles=None, name=f"q_proj_d{dil}")[0]
        for g, dil in enumerate(DILATIONS)]
    o = _attention(q_views, kv_views)
    xf = _proj_ln(o.reshape(t, d), xf, w_o, vec(ln1_g[1]), vec(ln1_b[1]), vec(ln1_g[1]), vec(ln1_b[1]),
                  conv_prologue=False, name="attn_out_ln1")
    xf, _ = _mlp(xf, up, down, vec(ln2_g[1]), vec(ln2_b[1]), layer=1, name="mlp1")
    xf = _ple(xf, pf, proj, gate, layer=1, name="ple1")
    return xf.reshape(b_, s, d)
```

```python
import functools

import jax
import jax.numpy as jnp
from jax import lax
from jax.experimental import pallas as pl
from jax.experimental.pallas import tpu as pltpu

D_MODEL = 2048
DEPTH = 2
HEAD_DIM = 128
N_HEADS = D_MODEL // HEAD_DIM
DILATIONS = (1, 4, 16)
N_GROUPS = len(DILATIONS)
ATTN_BLOCK = 128
ATTN_TILE = ATTN_BLOCK * DILATIONS[-1]
CONV_WIDTH = 31
CONV_HALO = 32
D_FF = 4 * D_MODEL
PLE_DIM = 256
ROPE_THETA = 10000.0
LN_EPS = 1e-5
DEEPNORM_ALPHA = (2 * DEPTH) ** 0.25

VMEM_LIMIT_BYTES = 58 * 1024 * 1024
BF16_SUBLANES = 16
BF16 = jnp.bfloat16
F32 = jnp.float32


def _params(*semantics):
    return pltpu.CompilerParams(dimension_semantics=semantics, vmem_limit_bytes=VMEM_LIMIT_BYTES)


def _layer_norm(v, g, b):
    mu = jnp.mean(v, axis=-1, keepdims=True)
    c = v - mu
    var = jnp.mean(c * c, axis=-1, keepdims=True)
    return c * lax.rsqrt(var + LN_EPS) * g + b


def _mm(a, b):
    return jnp.dot(a, b, preferred_element_type=F32)


def _split(refs, *sizes):
    out, at = [], 0
    for n in sizes:
        out.append(refs[at:at + n])
        at += n
    assert at == len(refs)
    return out


class _Riders:
    def __init__(self, arrays, steps, step_of_grid):
        self.shapes = [a.shape for a in arrays]
        self.operands = [a.reshape(-1, a.shape[-1]) for a in arrays]
        self.specs, self.out_shapes = [], []
        for a in self.operands:
            rows, width = a.shape[0] // steps, a.shape[1]
            assert rows * steps == a.shape[0] and rows % BF16_SUBLANES == 0
            self.specs.append(pl.BlockSpec((rows, width), lambda *g: (step_of_grid(*g), 0)))
            self.out_shapes.append(jax.ShapeDtypeStruct(a.shape, BF16))

    def restore(self, rounded):
        return [r.reshape(s) for r, s in zip(rounded, self.shapes)]


def _round_riders(in_refs, out_refs):
    for src, dst in zip(in_refs, out_refs):
        dst[...] = src[...].astype(BF16)


CONV_TM = 256
CONV_TN = 1024
CONV_ROWS = 16


def _glu_conv_kernel(x_ref, wa_ref, wg_ref, ba_ref, bg_ref, dw_ref, dwb_ref, *refs, n_riders):
    rider_in, (o_ref,), rider_out, (ubuf,) = _split(refs, n_riders, 1, n_riders, 1)
    _round_riders(rider_in, rider_out)
    i = pl.program_id(2)

    @pl.when(i == 0)
    def _():
        ubuf[pl.ds(0, CONV_HALO)] = jnp.zeros((CONV_HALO, 1, CONV_TN), F32)

    xb = x_ref[0].astype(BF16)
    a = _mm(xb, wa_ref[...]) + ba_ref[...]
    g = _mm(xb, wg_ref[...]) + bg_ref[...]
    ubuf[pl.ds(CONV_HALO, CONV_TM), 0, :] = a * jax.nn.sigmoid(g)

    taps = [dw_ref[k, 0] for k in range(CONV_WIDTH)]
    bias = dwb_ref[0, 0]
    first = CONV_HALO - (CONV_WIDTH - 1)

    def body(c, carry):
        t0 = c * CONV_ROWS
        acc = [bias] * CONV_ROWS
        for e in range(CONV_ROWS + CONV_WIDTH - 1):
            row = ubuf[t0 + first + e, 0]
            for r in range(max(0, e - CONV_WIDTH + 1), min(CONV_ROWS, e + 1)):
                acc[r] = acc[r] + taps[e - r] * row
        for r in range(CONV_ROWS):
            o_ref[0, t0 + r, 0] = acc[r]
        return carry

    lax.fori_loop(0, CONV_TM // CONV_ROWS, body, 0)
    ubuf[pl.ds(0, CONV_HALO)] = ubuf[pl.ds(CONV_TM, CONV_HALO)]


def _glu_conv(x, w_in, b_in, dw, dw_b, riders):
    b_, s, d = x.shape
    nj = d // CONV_TN
    ni = s // CONV_TM
    rider = _Riders(riders, nj * b_ * ni, lambda j, b, i: (j * b_ + b) * ni + i)
    conv, *rounded = pl.pallas_call(
        functools.partial(_glu_conv_kernel, n_riders=len(riders)),
        grid=(nj, b_, ni),
        in_specs=[
            pl.BlockSpec((1, CONV_TM, d), lambda j, b, i: (b, i, 0)),
            pl.BlockSpec((d, CONV_TN), lambda j, b, i: (0, j)),
            pl.BlockSpec((d, CONV_TN), lambda j, b, i: (0, j + nj)),
            pl.BlockSpec((1, CONV_TN), lambda j, b, i: (0, j)),
            pl.BlockSpec((1, CONV_TN), lambda j, b, i: (0, j + nj)),
            pl.BlockSpec((CONV_WIDTH, 1, CONV_TN), lambda j, b, i: (0, 0, j)),
            pl.BlockSpec((1, 1, CONV_TN), lambda j, b, i: (0, 0, j)),
            *rider.specs,
        ],
        out_specs=[pl.BlockSpec((1, CONV_TM, 1, CONV_TN), lambda j, b, i: (b, i, 0, j)), *rider.specs],
        out_shape=[jax.ShapeDtypeStruct((b_, s, 1, d), F32), *rider.out_shapes],
        scratch_shapes=[pltpu.VMEM((CONV_HALO + CONV_TM, 1, CONV_TN), F32)],
        compiler_params=_params("arbitrary", "arbitrary", "arbitrary"),
        name="glu_conv",
    )(x, w_in, w_in, b_in, b_in, dw, dw_b, *rider.operands)
    return conv, rider.restore(rounded)


ROW_TM = 512
ROW_SPLIT = 2


def _proj_ln_kernel(act_ref, x_ref, w_ref, pg_ref, pb_ref, g_ref, b_ref, o_ref, *scratch, conv_prologue):
    for h in range(ROW_SPLIT):
        rows = pl.ds(h * (ROW_TM // ROW_SPLIT), ROW_TM // ROW_SPLIT)
        if conv_prologue:
            a2d_ref, = scratch
            a2d_ref[rows, :] = act_ref[rows, 0, :]
            a = _layer_norm(a2d_ref[rows, :], pg_ref[...], pb_ref[...])
            a = a * jax.nn.sigmoid(a)
        else:
            a = act_ref[rows, :]
        mix = _mm(a.astype(BF16), w_ref[...])
        o_ref[rows, :] = _layer_norm(DEEPNORM_ALPHA * x_ref[rows, :] + mix, g_ref[...], b_ref[...])


def _proj_ln(act, x, w, pro_g, pro_b, g, b, *, conv_prologue, name):
    t, d = x.shape
    row = pl.BlockSpec((ROW_TM, d), lambda i: (i, 0))
    vec = pl.BlockSpec((1, d), lambda i: (0, 0))
    act_spec = pl.BlockSpec((ROW_TM, 1, d), lambda i: (i, 0, 0)) if conv_prologue else row
    return pl.pallas_call(
        functools.partial(_proj_ln_kernel, conv_prologue=conv_prologue),
        grid=(t // ROW_TM,),
        in_specs=[act_spec, row, pl.BlockSpec((d, d), lambda i: (0, 0)), vec, vec, vec, vec],
        out_specs=row,
        out_shape=jax.ShapeDtypeStruct((t, d), F32),
        scratch_shapes=[pltpu.VMEM((ROW_TM, d), F32)] if conv_prologue else [],
        compiler_params=_params("arbitrary"),
        name=name,
    )(act, x, w, pro_g, pro_b, g, b)


MLP_TM = 1024
MLP_TF = 512
MLP_LAST_ROWS = 256


def _mlp_kernel(x_ref, up_ref, down_ref, g_ref, b_ref, *refs, n_riders):
    rider_in, (o_ref,), rider_out, (xb_ref,) = _split(refs, n_riders, 1, n_riders, 1)
    _round_riders(rider_in, rider_out)
    k = pl.program_id(1)

    @pl.when(k == 0)
    def _():
        xb_ref[...] = x_ref[...].astype(BF16)
        o_ref[...] = jnp.zeros_like(o_ref)

    def partial_out(rows):
        h = jnp.maximum(_mm(xb_ref[rows, :], up_ref[...]), 0.0)
        return _mm((h * h).astype(BF16), down_ref[...])

    last = pl.num_programs(1) - 1

    @pl.when(k < last)
    def _():
        o_ref[...] += partial_out(slice(None))

    @pl.when(k == last)
    def _():
        for r0 in range(0, MLP_TM, MLP_LAST_ROWS):
            rows = pl.ds(r0, MLP_LAST_ROWS)
            y = o_ref[rows, :] + partial_out(rows)
            o_ref[rows, :] = _layer_norm(DEEPNORM_ALPHA * x_ref[rows, :] + y, g_ref[...], b_ref[...])


def _mlp(x, up, down, g, b, *, layer, name, riders=()):
    t, d = x.shape
    ff = up.shape[2]
    nk = ff // MLP_TF
    row = pl.BlockSpec((MLP_TM, d), lambda i, k: (i, 0))
    vec = pl.BlockSpec((1, d), lambda i, k: (0, 0))
    rider = _Riders(riders, (t // MLP_TM) * nk, lambda i, k: i * nk + k)
    out, *rounded = pl.pallas_call(
        functools.partial(_mlp_kernel, n_riders=len(riders)),
        grid=(t // MLP_TM, nk),
        in_specs=[row, pl.BlockSpec((None, d, MLP_TF), lambda i, k: (layer, 0, k)),
                  pl.BlockSpec((None, MLP_TF, d), lambda i, k: (layer, k, 0)), vec, vec, *rider.specs],
        out_specs=[row, *rider.specs],
        out_shape=[jax.ShapeDtypeStruct((t, d), F32), *rider.out_shapes],
        scratch_shapes=[pltpu.VMEM((MLP_TM, d), BF16)],
        compiler_params=_params("arbitrary", "arbitrary"),
        name=name,
    )(x, up, down, g, b, *rider.operands)
    return out, rider.restore(rounded)


def _ple_kernel(x_ref, p_ref, proj_ref, gate_ref, *refs, emit_mxu_inputs):
    if emit_mxu_inputs:
        g_ref, b_ref, o_ref, *mxu_inputs = refs
    else:
        (o_ref,), mxu_inputs = refs, ()
    for h in range(ROW_SPLIT):
        rows = pl.ds(h * (ROW_TM // ROW_SPLIT), ROW_TM // ROW_SPLIT)
        x = x_ref[rows, :]
        emb = _mm(p_ref[rows, :].astype(BF16), proj_ref[...])
        gate = jax.nn.sigmoid(_mm(x.astype(BF16), gate_ref[...]))
        out = x + emb * gate
        o_ref[rows, :] = out
        if mxu_inputs:
            plain_ref, normed_ref = mxu_inputs
            plain_ref[rows, :] = out.astype(BF16)
            normed_ref[rows, :] = _layer_norm(out, g_ref[...], b_ref[...]).astype(BF16)


def _ple(x, p, proj, gate, *, layer, name, kv_ln=None):
    t, d = x.shape
    row = pl.BlockSpec((ROW_TM, d), lambda i: (i, 0))
    vec = pl.BlockSpec((1, d), lambda i: (0, 0))
    emit = kv_ln is not None
    n_out = 3 if emit else 1
    return pl.pallas_call(
        functools.partial(_ple_kernel, emit_mxu_inputs=emit),
        grid=(t // ROW_TM,),
        in_specs=[row, pl.BlockSpec((None, ROW_TM, PLE_DIM), lambda i: (layer, i, 0)),
                  pl.BlockSpec((None, PLE_DIM, d), lambda i: (layer, 0, 0)),
                  pl.BlockSpec((None, d, d), lambda i: (layer, 0, 0))] + ([vec, vec] if emit else []),
        out_specs=[row] * n_out,
        out_shape=[jax.ShapeDtypeStruct((t, d), F32)] + [jax.ShapeDtypeStruct((t, d), BF16)] * (n_out - 1),
        compiler_params=_params("arbitrary"),
        name=name,
    )(x, p, proj, gate, *(kv_ln if emit else ()))


ROPE_TM = 512


def _rope_table_kernel(pos_ref, freq_ref, cos_ref, sin_ref):
    half_rows = ROPE_TM // 2
    low = lax.broadcasted_iota(jnp.int32, (half_rows, HEAD_DIM), 1) < HEAD_DIM // 2
    ang = jnp.where(low, pos_ref[:, 0:1], pos_ref[:, 1:2]) * freq_ref[...]
    c, s = jnp.cos(ang), jnp.sin(ang)
    c_sw, s_sw = pltpu.roll(c, HEAD_DIM // 2, axis=1), pltpu.roll(s, HEAD_DIM // 2, axis=1)
    even, odd = pl.ds(0, half_rows, stride=2), pl.ds(1, half_rows, stride=2)
    cos_ref[even, :] = jnp.where(low, c, c_sw)
    cos_ref[odd, :] = jnp.where(low, c_sw, c)
    sin_ref[even, :] = jnp.where(low, -s, s_sw)
    sin_ref[odd, :] = jnp.where(low, -s_sw, s)


def _rope_tables(positions):
    t = positions.size
    half = HEAD_DIM // 2
    inv_freq = ROPE_THETA ** (-jnp.arange(half, dtype=F32) * (2.0 / HEAD_DIM))
    freq = jnp.concatenate([inv_freq, inv_freq]).reshape(1, HEAD_DIM)
    pos = positions.astype(F32).reshape(t // 2, 2)
    tab = pl.BlockSpec((ROPE_TM, HEAD_DIM), lambda i: (i, 0))
    return pl.pallas_call(
        _rope_table_kernel,
        grid=(t // ROPE_TM,),
        in_specs=[pl.BlockSpec((ROPE_TM // 2, 2), lambda i: (i, 0)), pl.BlockSpec((1, HEAD_DIM), lambda i: (0, 0))],
        out_specs=[tab, tab],
        out_shape=[jax.ShapeDtypeStruct((t, HEAD_DIM), F32)] * 2,
        compiler_params=_params("arbitrary"),
        name="rope_tables",
    )(pos, freq)


def _rotary(v, cos, sin_signed):
    return v * cos + pltpu.roll(v, HEAD_DIM // 2, axis=1) * sin_signed


HEADS_TM = 1024
HEADS_TN = 1024
HEADS_PER_TILE = HEADS_TN // HEAD_DIM
HEADS_CHUNK = 256
HEADS_SUB_M = 256


def _heads_kernel(x_ref, w_ref, cos_ref, sin_ref, *refs, dilations, n_rotary_tiles):
    assert set(dilations) <= {1, 4, 16}
    outs = dict(zip(dilations, refs[:len(dilations)]))
    stage_ref, stage2_ref = refs[len(dilations):]
    quarter = HEADS_SUB_M // 4
    j = pl.program_id(2)

    cos, sin = cos_ref[...], sin_ref[...]
    if n_rotary_tiles is not None:
        rotate = j < n_rotary_tiles
        cos = jnp.where(rotate, cos, 1.0)
        sin = jnp.where(rotate, sin, 0.0)

    for c in range(HEADS_TN // HEADS_CHUNK):
        for m0 in range(0, HEADS_TM, HEADS_SUB_M):
            res = _mm(x_ref[0, pl.ds(m0, HEADS_SUB_M), :], w_ref[:, c * HEADS_CHUNK:(c + 1) * HEADS_CHUNK])
            for ch in range(HEADS_CHUNK // HEAD_DIM):
                hh = c * (HEADS_CHUNK // HEAD_DIM) + ch
                v = _rotary(res[:, ch * HEAD_DIM:(ch + 1) * HEAD_DIM],
                            cos[m0:m0 + HEADS_SUB_M], sin[m0:m0 + HEADS_SUB_M])
                if 1 in outs:
                    outs[1][0, hh, pl.ds(m0, HEADS_SUB_M), :] = v.astype(BF16)
                if max(dilations) == 1:
                    continue
                stage_ref[hh, pl.ds(m0, HEADS_SUB_M), :] = v
                for r4 in range(4):
                    part = stage_ref[hh, pl.ds(m0 + r4, quarter, stride=4), :]
                    if 4 in outs:
                        outs[4][0, hh, pl.ds(m0 // 4, quarter), r4 * HEAD_DIM:(r4 + 1) * HEAD_DIM] = part.astype(BF16)
                    if 16 not in outs:
                        continue
                    stage2_ref[hh, pl.ds(m0 + r4 * quarter, quarter), :] = part
                    for q4 in range(4):
                        r16 = 4 * q4 + r4
                        sub = stage2_ref[hh, pl.ds(m0 + r4 * quarter + q4, quarter // 4, stride=4), :]
                        outs[16][0, hh, pl.ds(m0 // 16, quarter // 4), r16 * HEAD_DIM:(r16 + 1) * HEAD_DIM] = (
                            sub.astype(BF16))


def _heads_proj(x, w, cos, sin, *, n_out, col_tile_offset, dilations, n_rotary_tiles, name):
    b_, s, d = x.shape
    per_b = s // HEADS_TM
    tab = pl.BlockSpec((HEADS_TM, HEAD_DIM), lambda bb, i, j: (bb * per_b + i, 0))
    n_heads = n_out // HEAD_DIM
    return pl.pallas_call(
        functools.partial(_heads_kernel, dilations=dilations, n_rotary_tiles=n_rotary_tiles),
        grid=(b_, per_b, n_out // HEADS_TN),
        in_specs=[pl.BlockSpec((1, HEADS_TM, d), lambda bb, i, j: (bb, i, 0)),
                  pl.BlockSpec((d, HEADS_TN), lambda bb, i, j: (0, col_tile_offset + j)), tab, tab],
        out_specs=[pl.BlockSpec((1, HEADS_PER_TILE, HEADS_TM // dil, dil * HEAD_DIM), lambda bb, i, j: (bb, j, i, 0))
                   for dil in dilations],
        out_shape=[jax.ShapeDtypeStruct((b_, n_heads, s // dil, dil * HEAD_DIM), BF16) for dil in dilations],
        scratch_shapes=[pltpu.VMEM((HEADS_PER_TILE, HEADS_TM, HEAD_DIM), F32)] * 2,
        compiler_params=_params("arbitrary", "arbitrary", "arbitrary"),
        name=name,
    )(x, w, cos, sin)


MERGE_ROWS = 256
LOG2_E = 1.4426950408889634


def _attn_kernel(*refs):
    group_refs = [refs[5 * g:5 * g + 5] for g in range(N_GROUPS)]
    o_ref, og_ref, lse_ref, bias_ref = refs[5 * N_GROUPS:]
    tile = pl.program_id(2)
    scale = HEAD_DIM ** -0.5
    nt_dims = (((1,), (1,)), ((), ()))

    qi = lax.broadcasted_iota(jnp.int32, (ATTN_BLOCK, 2 * ATTN_BLOCK), 0)
    kj = lax.broadcasted_iota(jnp.int32, (ATTN_BLOCK, 2 * ATTN_BLOCK), 1)
    bias = jnp.where(jnp.logical_and(kj >= qi, kj <= qi + ATTN_BLOCK), 0.0, -jnp.inf)
    bias_ref[0] = bias
    bias_ref[1] = jnp.where(jnp.logical_or(tile > 0, kj >= ATTN_BLOCK), bias, -jnp.inf)

    for g, (dil, (q_ref, kc_ref, kp_ref, vc_ref, vp_ref)) in enumerate(zip(DILATIONS, group_refs)):
        n_blocks = ATTN_TILE // (ATTN_BLOCK * dil)
        for r in range(dil):
            lanes = slice(r * HEAD_DIM, (r + 1) * HEAD_DIM)
            k_prev = kp_ref[0, 0, :, lanes]
            v_prev = vp_ref[0, 0, :, lanes]
            for n in range(n_blocks):
                rows = slice(n * ATTN_BLOCK, (n + 1) * ATTN_BLOCK)
                q = q_ref[0, 0, rows, lanes]
                k_cur = kc_ref[0, 0, rows, lanes]
                v_cur = vc_ref[0, 0, rows, lanes]
                keys = jnp.concatenate([k_prev, k_cur], axis=0)
                vals = jnp.concatenate([v_prev, v_cur], axis=0)
                s = lax.dot_general(q, keys, nt_dims, preferred_element_type=F32) + bias_ref[1 if n == 0 else 0]
                m = jnp.max(s, axis=1, keepdims=True)
                p = jnp.exp2((s - m) * (scale * LOG2_E))
                l = jnp.sum(p, axis=1, keepdims=True)
                pv = _mm(p.astype(BF16), vals)
                if dil == 1:
                    dst = pl.ds(n * ATTN_BLOCK, ATTN_BLOCK)
                else:
                    dst = pl.ds(n * ATTN_BLOCK * dil + r, ATTN_BLOCK, stride=dil)
                og_ref[g, dst, :] = pv / l
                lse_ref[g, dst, :] = jnp.broadcast_to(m * scale + jnp.log(l), (ATTN_BLOCK, HEAD_DIM))
                k_prev, v_prev = k_cur, v_cur

    for c in range(ATTN_TILE // MERGE_ROWS):
        rows = slice(c * MERGE_ROWS, (c + 1) * MERGE_ROWS)
        lses = [lse_ref[g, rows, :] for g in range(N_GROUPS)]
        top = functools.reduce(jnp.maximum, lses)
        wts = [jnp.exp(v - top) for v in lses]
        num = sum(w * og_ref[g, rows, :] for g, w in enumerate(wts))
        o_ref[0, rows, :] = (num / sum(wts)).astype(o_ref.dtype)


def _attention(q_views, kv_views):
    hd = HEAD_DIM
    b_ = q_views[0].shape[0]
    s = q_views[0].shape[2]
    n_tiles = s // ATTN_TILE
    operands, in_specs = [], []
    for dil, qv, kvv in zip(DILATIONS, q_views, kv_views):
        rows, width = ATTN_TILE // dil, dil * hd
        prev_per_tile = rows // ATTN_BLOCK

        def cur(head_offset):
            return pl.BlockSpec((1, 1, rows, width), lambda b, h, t, o=head_offset: (b, o + h, t, 0))

        def prev(head_offset):
            return pl.BlockSpec((1, 1, ATTN_BLOCK, width),
                                lambda b, h, t, o=head_offset, n=prev_per_tile: (b, o + h, jnp.maximum(n * t - 1, 0), 0))

        operands += [qv, kvv, kvv, kvv, kvv]
        in_specs += [cur(0), cur(0), prev(0), cur(N_HEADS), prev(N_HEADS)]

    return pl.pallas_call(
        _attn_kernel,
        grid=(b_, N_HEADS, n_tiles),
        in_specs=in_specs,
        out_specs=pl.BlockSpec((1, ATTN_TILE, hd), lambda b, h, t: (b, t, h)),
        out_shape=jax.ShapeDtypeStruct((b_, s, N_HEADS * hd), BF16),
        scratch_shapes=[pltpu.VMEM((N_GROUPS, ATTN_TILE, hd), F32), pltpu.VMEM((N_GROUPS, ATTN_TILE, hd), F32),
                        pltpu.VMEM((2, ATTN_BLOCK, 2 * ATTN_BLOCK), F32)],
        compiler_params=_params("arbitrary", "arbitrary", "arbitrary"),
        name="dilated_attention",
    )(*operands)


def kernel(x, p, positions, conv_w_in, conv_b_in, conv_dw, conv_dw_b, conv_ln_g, conv_ln_b, conv_w_out,
           kv_ln_g, kv_ln_b, w_kv, attn_w_q, attn_w_o, ln1_g, ln1_b, mlp_up, mlp_down, ln2_g, ln2_b,
           ple_proj, ple_gate):
    b_, s, d = x.shape
    t = b_ * s
    assert s % ATTN_TILE == 0 and d == D_MODEL

    def vec(v):
        return v.reshape(1, -1)

    def wb(w):
        return w.astype(BF16)

    cos, sin = _rope_tables(positions)

    pf = p.reshape(DEPTH, t, PLE_DIM)
    conv, (up, down, w_out) = _glu_conv(x, wb(conv_w_in[0]), vec(conv_b_in[0]), conv_dw[0],
                                        conv_dw_b[0].reshape(1, 1, d), [mlp_up, mlp_down, conv_w_out[0]])
    xf = x.reshape(t, d)
    xf = _proj_ln(conv.reshape(t, 1, d), xf, w_out, vec(conv_ln_g[0]), vec(conv_ln_b[0]),
                  vec(ln1_g[0]), vec(ln1_b[0]), conv_prologue=True, name="conv_out_ln1")
    xf, (gate, w_kv_b, w_q, w_o) = _mlp(xf, up, down, vec(ln2_g[0]), vec(ln2_b[0]), layer=0, name="mlp0",
                                        riders=[ple_gate, w_kv, attn_w_q[0], attn_w_o[0]])
    proj = wb(ple_proj)
    xf, x_plain, x_normed = _ple(xf, pf, proj, gate, layer=0, name="ple0", kv_ln=(vec(kv_ln_g), vec(kv_ln_b)))

    kv_views = _heads_proj(x_normed.reshape(b_, s, d), w_kv_b, cos, sin, n_out=2 * d, col_tile_offset=0,
                           dilations=DILATIONS, n_rotary_tiles=d // HEADS_TN, name="kv_proj")
    q_views = [
        _heads_proj(x_plain.reshape(b_, s, d), w_q, cos, sin, n_out=d, col_tile_offset=g * (d // HEADS_TN),
                    dilations=(dil,), n_rotary_tiles=None, name=f"q_proj_d{dil}")[0]
        for g, dil in enumerate(DILATIONS)]
    o = _attention(q_views, kv_views)
    xf = _proj_ln(o.reshape(t, d), xf, w_o, vec(ln1_g[1]), vec(ln1_b[1]), vec(ln1_g[1]), vec(ln1_b[1]),
                  conv_prologue=False, name="attn_out_ln1")
    xf, _ = _mlp(xf, up, down, vec(ln2_g[1]), vec(ln2_b[1]), layer=1, name="mlp1")
    xf, = _ple(xf, pf, proj, gate, layer=1, name="ple1")
    return xf.reshape(b_, s, d)
```

```python
import functools

import jax
import jax.numpy as jnp
from jax import lax
from jax.experimental import pallas as pl
from jax.experimental.pallas import tpu as pltpu

D_MODEL = 2048
DEPTH = 2
HEAD_DIM = 128
N_HEADS = D_MODEL // HEAD_DIM
DILATIONS = (1, 4, 16)
N_GROUPS = len(DILATIONS)
ATTN_BLOCK = 128
ATTN_TILE = ATTN_BLOCK * DILATIONS[-1]
CONV_WIDTH = 31
CONV_HALO = 32
D_FF = 4 * D_MODEL
PLE_DIM = 256
ROPE_THETA = 10000.0
LN_EPS = 1e-5
DEEPNORM_ALPHA = (2 * DEPTH) ** 0.25

VMEM_LIMIT_BYTES = 58 * 1024 * 1024
BF16_SUBLANES = 16
BF16 = jnp.bfloat16
F32 = jnp.float32


def _params(*semantics):
    return pltpu.CompilerParams(dimension_semantics=semantics, vmem_limit_bytes=VMEM_LIMIT_BYTES)


def _layer_norm(v, g, b):
    mu = jnp.mean(v, axis=-1, keepdims=True)
    c = v - mu
    var = jnp.mean(c * c, axis=-1, keepdims=True)
    return c * lax.rsqrt(var + LN_EPS) * g + b


def _mm(a, b):
    return jnp.dot(a, b, preferred_element_type=F32)


def _split(refs, *sizes):
    out, at = [], 0
    for n in sizes:
        out.append(refs[at:at + n])
        at += n
    assert at == len(refs)
    return out


class _Riders:
    def __init__(self, arrays, steps, step_of_grid):
        self.shapes = [a.shape for a in arrays]
        self.operands = [a.reshape(-1, a.shape[-1]) for a in arrays]
        self.specs, self.out_shapes = [], []
        for a in self.operands:
            rows, width = a.shape[0] // steps, a.shape[1]
            assert rows * steps == a.shape[0] and rows % BF16_SUBLANES == 0
            self.specs.append(pl.BlockSpec((rows, width), lambda *g: (step_of_grid(*g), 0)))
            self.out_shapes.append(jax.ShapeDtypeStruct(a.shape, BF16))

    def restore(self, rounded):
        return [r.reshape(s) for r, s in zip(rounded, self.shapes)]


def _round_riders(in_refs, out_refs):
    for src, dst in zip(in_refs, out_refs):
        dst[...] = src[...].astype(BF16)


CONV_TM = 256
CONV_TN = 1024
CONV_ROWS = 16


def _glu_conv_kernel(x_ref, wa_ref, wg_ref, ba_ref, bg_ref, dw_ref, dwb_ref, *refs, n_riders):
    rider_in, (o_ref,), rider_out, (ubuf,) = _split(refs, n_riders, 1, n_riders, 1)
    _round_riders(rider_in, rider_out)
    i = pl.program_id(2)

    @pl.when(i == 0)
    def _():
        ubuf[pl.ds(0, CONV_HALO)] = jnp.zeros((CONV_HALO, 1, CONV_TN), F32)

    xb = x_ref[0].astype(BF16)
    a = _mm(xb, wa_ref[...]) + ba_ref[...]
    g = _mm(xb, wg_ref[...]) + bg_ref[...]
    ubuf[pl.ds(CONV_HALO, CONV_TM), 0, :] = a * jax.nn.sigmoid(g)

    taps = [dw_ref[k, 0] for k in range(CONV_WIDTH)]
    bias = dwb_ref[0, 0]
    first = CONV_HALO - (CONV_WIDTH - 1)

    def body(c, carry):
        t0 = c * CONV_ROWS
        acc = [bias] * CONV_ROWS
        for e in range(CONV_ROWS + CONV_WIDTH - 1):
            row = ubuf[t0 + first + e, 0]
            for r in range(max(0, e - CONV_WIDTH + 1), min(CONV_ROWS, e + 1)):
                acc[r] = acc[r] + taps[e - r] * row
        for r in range(CONV_ROWS):
            o_ref[0, t0 + r, 0] = acc[r]
        return carry

    lax.fori_loop(0, CONV_TM // CONV_ROWS, body, 0, unroll=4)
    ubuf[pl.ds(0, CONV_HALO)] = ubuf[pl.ds(CONV_TM, CONV_HALO)]


def _glu_conv(x, w_in, b_in, dw, dw_b, riders):
    b_, s, d = x.shape
    nj = d // CONV_TN
    ni = s // CONV_TM
    rider = _Riders(riders, nj * b_ * ni, lambda j, b, i: (j * b_ + b) * ni + i)
    conv, *rounded = pl.pallas_call(
        functools.partial(_glu_conv_kernel, n_riders=len(riders)),
        grid=(nj, b_, ni),
        in_specs=[
            pl.BlockSpec((1, CONV_TM, d), lambda j, b, i: (b, i, 0)),
            pl.BlockSpec((d, CONV_TN), lambda j, b, i: (0, j)),
            pl.BlockSpec((d, CONV_TN), lambda j, b, i: (0, j + nj)),
            pl.BlockSpec((1, CONV_TN), lambda j, b, i: (0, j)),
            pl.BlockSpec((1, CONV_TN), lambda j, b, i: (0, j + nj)),
            pl.BlockSpec((CONV_WIDTH, 1, CONV_TN), lambda j, b, i: (0, 0, j)),
            pl.BlockSpec((1, 1, CONV_TN), lambda j, b, i: (0, 0, j)),
            *rider.specs,
        ],
        out_specs=[pl.BlockSpec((1, CONV_TM, 1, CONV_TN), lambda j, b, i: (b, i, 0, j)), *rider.specs],
        out_shape=[jax.ShapeDtypeStruct((b_, s, 1, d), F32), *rider.out_shapes],
        scratch_shapes=[pltpu.VMEM((CONV_HALO + CONV_TM, 1, CONV_TN), F32)],
        compiler_params=_params("arbitrary", "arbitrary", "arbitrary"),
        name="glu_conv",
    )(x, w_in, w_in, b_in, b_in, dw, dw_b, *rider.operands)
    return conv, rider.restore(rounded)


ROW_TM = 512
ROW_SPLIT = 2


def _proj_ln_kernel(act_ref, x_ref, w_ref, pg_ref, pb_ref, g_ref, b_ref, o_ref, *scratch, conv_prologue):
    for h in range(ROW_SPLIT):
        rows = pl.ds(h * (ROW_TM // ROW_SPLIT), ROW_TM // ROW_SPLIT)
        if conv_prologue:
            a2d_ref, = scratch
            a2d_ref[rows, :] = act_ref[rows, 0, :]
            a = _layer_norm(a2d_ref[rows, :], pg_ref[...], pb_ref[...])
            a = a * jax.nn.sigmoid(a)
        else:
            a = act_ref[rows, :]
        mix = _mm(a.astype(BF16), w_ref[...])
        o_ref[rows, :] = _layer_norm(DEEPNORM_ALPHA * x_ref[rows, :] + mix, g_ref[...], b_ref[...])


def _proj_ln(act, x, w, pro_g, pro_b, g, b, *, conv_prologue, name):
    t, d = x.shape
    row = pl.BlockSpec((ROW_TM, d), lambda i: (i, 0))
    vec = pl.BlockSpec((1, d), lambda i: (0, 0))
    act_spec = pl.BlockSpec((ROW_TM, 1, d), lambda i: (i, 0, 0)) if conv_prologue else row
    return pl.pallas_call(
        functools.partial(_proj_ln_kernel, conv_prologue=conv_prologue),
        grid=(t // ROW_TM,),
        in_specs=[act_spec, row, pl.BlockSpec((d, d), lambda i: (0, 0)), vec, vec, vec, vec],
        out_specs=row,
        out_shape=jax.ShapeDtypeStruct((t, d), F32),
        scratch_shapes=[pltpu.VMEM((ROW_TM, d), F32)] if conv_prologue else [],
        compiler_params=_params("arbitrary"),
        name=name,
    )(act, x, w, pro_g, pro_b, g, b)


MLP_TM = 1024
MLP_TF = 512
MLP_LAST_ROWS = 256


def _mlp_kernel(x_ref, up_ref, down_ref, g_ref, b_ref, *refs, n_riders):
    rider_in, (o_ref,), rider_out, (xb_ref,) = _split(refs, n_riders, 1, n_riders, 1)
    _round_riders(rider_in, rider_out)
    k = pl.program_id(1)

    @pl.when(k == 0)
    def _():
        xb_ref[...] = x_ref[...].astype(BF16)
        o_ref[...] = jnp.zeros_like(o_ref)

    def partial_out(rows):
        h = jnp.maximum(_mm(xb_ref[rows, :], up_ref[...]), 0.0)
        return _mm((h * h).astype(BF16), down_ref[...])

    last = pl.num_programs(1) - 1

    @pl.when(k < last)
    def _():
        o_ref[...] += partial_out(slice(None))

    @pl.when(k == last)
    def _():
        for r0 in range(0, MLP_TM, MLP_LAST_ROWS):
            rows = pl.ds(r0, MLP_LAST_ROWS)
            y = o_ref[rows, :] + partial_out(rows)
            o_ref[rows, :] = _layer_norm(DEEPNORM_ALPHA * x_ref[rows, :] + y, g_ref[...], b_ref[...])


def _mlp(x, up, down, g, b, *, layer, name, riders=()):
    t, d = x.shape
    ff = up.shape[2]
    nk = ff // MLP_TF
    row = pl.BlockSpec((MLP_TM, d), lambda i, k: (i, 0))
    vec = pl.BlockSpec((1, d), lambda i, k: (0, 0))
    rider = _Riders(riders, (t // MLP_TM) * nk, lambda i, k: i * nk + k)
    out, *rounded = pl.pallas_call(
        functools.partial(_mlp_kernel, n_riders=len(riders)),
        grid=(t // MLP_TM, nk),
        in_specs=[row, pl.BlockSpec((None, d, MLP_TF), lambda i, k: (layer, 0, k)),
                  pl.BlockSpec((None, MLP_TF, d), lambda i, k: (layer, k, 0)), vec, vec, *rider.specs],
        out_specs=[row, *rider.specs],
        out_shape=[jax.ShapeDtypeStruct((t, d), F32), *rider.out_shapes],
        scratch_shapes=[pltpu.VMEM((MLP_TM, d), BF16)],
        compiler_params=_params("arbitrary", "arbitrary"),
        name=name,
    )(x, up, down, g, b, *rider.operands)
    return out, rider.restore(rounded)


def _ple_kernel(x_ref, p_ref, proj_ref, gate_ref, *refs, emit_mxu_inputs):
    if emit_mxu_inputs:
        g_ref, b_ref, o_ref, *mxu_inputs = refs
    else:
        (o_ref,), mxu_inputs = refs, ()
    for h in range(ROW_SPLIT):
        rows = pl.ds(h * (ROW_TM // ROW_SPLIT), ROW_TM // ROW_SPLIT)
        x = x_ref[rows, :]
        emb = _mm(p_ref[rows, :].astype(BF16), proj_ref[...])
        gate = jax.nn.sigmoid(_mm(x.astype(BF16), gate_ref[...]))
        out = x + emb * gate
        o_ref[rows, :] = out
        if mxu_inputs:
            plain_ref, normed_ref = mxu_inputs
            plain_ref[rows, :] = out.astype(BF16)
            normed_ref[rows, :] = _layer_norm(out, g_ref[...], b_ref[...]).astype(BF16)


def _ple(x, p, proj, gate, *, layer, name, kv_ln=None):
    t, d = x.shape
    row = pl.BlockSpec((ROW_TM, d), lambda i: (i, 0))
    vec = pl.BlockSpec((1, d), lambda i: (0, 0))
    emit = kv_ln is not None
    n_out = 3 if emit else 1
    return pl.pallas_call(
        functools.partial(_ple_kernel, emit_mxu_inputs=emit),
        grid=(t // ROW_TM,),
        in_specs=[row, pl.BlockSpec((None, ROW_TM, PLE_DIM), lambda i: (layer, i, 0)),
                  pl.BlockSpec((None, PLE_DIM, d), lambda i: (layer, 0, 0)),
                  pl.BlockSpec((None, d, d), lambda i: (layer, 0, 0))] + ([vec, vec] if emit else []),
        out_specs=[row] * n_out,
        out_shape=[jax.ShapeDtypeStruct((t, d), F32)] + [jax.ShapeDtypeStruct((t, d), BF16)] * (n_out - 1),
        compiler_params=_params("arbitrary"),
        name=name,
    )(x, p, proj, gate, *(kv_ln if emit else ()))


ROPE_TM = 512


def _rope_table_kernel(pos_ref, freq_ref, cos_ref, sin_ref):
    half_rows = ROPE_TM // 2
    low = lax.broadcasted_iota(jnp.int32, (half_rows, HEAD_DIM), 1) < HEAD_DIM // 2
    ang = jnp.where(low, pos_ref[:, 0:1], pos_ref[:, 1:2]) * freq_ref[...]
    c, s = jnp.cos(ang), jnp.sin(ang)
    c_sw, s_sw = pltpu.roll(c, HEAD_DIM // 2, axis=1), pltpu.roll(s, HEAD_DIM // 2, axis=1)
    even, odd = pl.ds(0, half_rows, stride=2), pl.ds(1, half_rows, stride=2)
    cos_ref[even, :] = jnp.where(low, c, c_sw)
    cos_ref[odd, :] = jnp.where(low, c_sw, c)
    sin_ref[even, :] = jnp.where(low, -s, s_sw)
    sin_ref[odd, :] = jnp.where(low, -s_sw, s)


def _rope_tables(positions):
    t = positions.size
    half = HEAD_DIM // 2
    inv_freq = ROPE_THETA ** (-jnp.arange(half, dtype=F32) * (2.0 / HEAD_DIM))
    freq = jnp.concatenate([inv_freq, inv_freq]).reshape(1, HEAD_DIM)
    pos = positions.astype(F32).reshape(t // 2, 2)
    tab = pl.BlockSpec((ROPE_TM, HEAD_DIM), lambda i: (i, 0))
    return pl.pallas_call(
        _rope_table_kernel,
        grid=(t // ROPE_TM,),
        in_specs=[pl.BlockSpec((ROPE_TM // 2, 2), lambda i: (i, 0)), pl.BlockSpec((1, HEAD_DIM), lambda i: (0, 0))],
        out_specs=[tab, tab],
        out_shape=[jax.ShapeDtypeStruct((t, HEAD_DIM), F32)] * 2,
        compiler_params=_params("arbitrary"),
        name="rope_tables",
    )(pos, freq)


def _rotary(v, cos, sin_signed):
    return v * cos + pltpu.roll(v, HEAD_DIM // 2, axis=1) * sin_signed


HEADS_TM = 1024
HEADS_TN = 1024
HEADS_PER_TILE = HEADS_TN // HEAD_DIM
HEADS_CHUNK = 256
HEADS_SUB_M = 256


def _heads_kernel(x_ref, w_ref, cos_ref, sin_ref, *refs, dilations, n_rotary_tiles):
    assert set(dilations) <= {1, 4, 16}
    outs = dict(zip(dilations, refs[:len(dilations)]))
    stage_ref, stage2_ref = refs[len(dilations):]
    quarter = HEADS_SUB_M // 4
    j = pl.program_id(2)

    cos, sin = cos_ref[...], sin_ref[...]
    if n_rotary_tiles is not None:
        rotate = j < n_rotary_tiles
        cos = jnp.where(rotate, cos, 1.0)
        sin = jnp.where(rotate, sin, 0.0)

    for c in range(HEADS_TN // HEADS_CHUNK):
        for m0 in range(0, HEADS_TM, HEADS_SUB_M):
            res = _mm(x_ref[0, pl.ds(m0, HEADS_SUB_M), :], w_ref[:, c * HEADS_CHUNK:(c + 1) * HEADS_CHUNK])
            for ch in range(HEADS_CHUNK // HEAD_DIM):
                hh = c * (HEADS_CHUNK // HEAD_DIM) + ch
                v = _rotary(res[:, ch * HEAD_DIM:(ch + 1) * HEAD_DIM],
                            cos[m0:m0 + HEADS_SUB_M], sin[m0:m0 + HEADS_SUB_M])
                if 1 in outs:
                    outs[1][0, hh, pl.ds(m0, HEADS_SUB_M), :] = v.astype(BF16)
                if max(dilations) == 1:
                    continue
                stage_ref[hh, pl.ds(m0, HEADS_SUB_M), :] = v
                for r4 in range(4):
                    part = stage_ref[hh, pl.ds(m0 + r4, quarter, stride=4), :]
                    if 4 in outs:
                        outs[4][0, hh, pl.ds(m0 // 4, quarter), r4 * HEAD_DIM:(r4 + 1) * HEAD_DIM] = part.astype(BF16)
                    if 16 not in outs:
                        continue
                    stage2_ref[hh, pl.ds(m0 + r4 * quarter, quarter), :] = part
                    for q4 in range(4):
                        r16 = 4 * q4 + r4
                        sub = stage2_ref[hh, pl.ds(m0 + r4 * quarter + q4, quarter // 4, stride=4), :]
                        outs[16][0, hh, pl.ds(m0 // 16, quarter // 4), r16 * HEAD_DIM:(r16 + 1) * HEAD_DIM] = (
                            sub.astype(BF16))


def _heads_proj(x, w, cos, sin, *, n_out, col_tile_offset, dilations, n_rotary_tiles, name):
    b_, s, d = x.shape
    per_b = s // HEADS_TM
    tab = pl.BlockSpec((HEADS_TM, HEAD_DIM), lambda bb, i, j: (bb * per_b + i, 0))
    n_heads = n_out // HEAD_DIM
    return pl.pallas_call(
        functools.partial(_heads_kernel, dilations=dilations, n_rotary_tiles=n_rotary_tiles),
        grid=(b_, per_b, n_out // HEADS_TN),
        in_specs=[pl.BlockSpec((1, HEADS_TM, d), lambda bb, i, j: (bb, i, 0)),
                  pl.BlockSpec((d, HEADS_TN), lambda bb, i, j: (0, col_tile_offset + j)), tab, tab],
        out_specs=[pl.BlockSpec((1, HEADS_PER_TILE, HEADS_TM // dil, dil * HEAD_DIM), lambda bb, i, j: (bb, j, i, 0))
                   for dil in dilations],
        out_shape=[jax.ShapeDtypeStruct((b_, n_heads, s // dil, dil * HEAD_DIM), BF16) for dil in dilations],
        scratch_shapes=[pltpu.VMEM((HEADS_PER_TILE, HEADS_TM, HEAD_DIM), F32)] * 2,
        compiler_params=_params("arbitrary", "arbitrary", "arbitrary"),
        name=name,
    )(x, w, cos, sin)


MERGE_ROWS = 256
LOG2_E = 1.4426950408889634


def _attn_kernel(*refs):
    group_refs = [refs[5 * g:5 * g + 5] for g in range(N_GROUPS)]
    o_ref, og_ref, lse_ref, bias_ref = refs[5 * N_GROUPS:]
    tile = pl.program_id(2)
    scale = HEAD_DIM ** -0.5
    nt_dims = (((1,), (1,)), ((), ()))

    qi = lax.broadcasted_iota(jnp.int32, (ATTN_BLOCK, 2 * ATTN_BLOCK), 0)
    kj = lax.broadcasted_iota(jnp.int32, (ATTN_BLOCK, 2 * ATTN_BLOCK), 1)
    bias = jnp.where(jnp.logical_and(kj >= qi, kj <= qi + ATTN_BLOCK), 0.0, -jnp.inf)
    bias_ref[0] = bias
    bias_ref[1] = jnp.where(jnp.logical_or(tile > 0, kj >= ATTN_BLOCK), bias, -jnp.inf)

    for g, (dil, (q_ref, kc_ref, kp_ref, vc_ref, vp_ref)) in enumerate(zip(DILATIONS, group_refs)):
        n_blocks = ATTN_TILE // (ATTN_BLOCK * dil)
        for r in range(dil):
            lanes = slice(r * HEAD_DIM, (r + 1) * HEAD_DIM)
            k_prev = kp_ref[0, 0, :, lanes]
            v_prev = vp_ref[0, 0, :, lanes]
            for n in range(n_blocks):
                rows = slice(n * ATTN_BLOCK, (n + 1) * ATTN_BLOCK)
                q = q_ref[0, 0, rows, lanes]
                k_cur = kc_ref[0, 0, rows, lanes]
                v_cur = vc_ref[0, 0, rows, lanes]
                keys = jnp.concatenate([k_prev, k_cur], axis=0)
                vals = jnp.concatenate([v_prev, v_cur], axis=0)
                s = lax.dot_general(q, keys, nt_dims, preferred_element_type=F32) + bias_ref[1 if n == 0 else 0]
                m = jnp.max(s, axis=1, keepdims=True)
                p = jnp.exp2((s - m) * (scale * LOG2_E))
                l = jnp.sum(p, axis=1, keepdims=True)
                pv = _mm(p.astype(BF16), vals)
                if dil == 1:
                    dst = pl.ds(n * ATTN_BLOCK, ATTN_BLOCK)
                else:
                    dst = pl.ds(n * ATTN_BLOCK * dil + r, ATTN_BLOCK, stride=dil)
                og_ref[g, dst, :] = pv / l
                lse_ref[g, dst, :] = jnp.broadcast_to(m * scale + jnp.log(l), (ATTN_BLOCK, HEAD_DIM))
                k_prev, v_prev = k_cur, v_cur

    for c in range(ATTN_TILE // MERGE_ROWS):
        rows = slice(c * MERGE_ROWS, (c + 1) * MERGE_ROWS)
        lses = [lse_ref[g, rows, :] for g in range(N_GROUPS)]
        top = functools.reduce(jnp.maximum, lses)
        wts = [jnp.exp(v - top) for v in lses]
        num = sum(w * og_ref[g, rows, :] for g, w in enumerate(wts))
        o_ref[0, rows, :] = (num / sum(wts)).astype(o_ref.dtype)


def _attention(q_views, kv_views):
    hd = HEAD_DIM
    b_ = q_views[0].shape[0]
    s = q_views[0].shape[2]
    n_tiles = s // ATTN_TILE
    operands, in_specs = [], []
    for dil, qv, kvv in zip(DILATIONS, q_views, kv_views):
        rows, width = ATTN_TILE // dil, dil * hd
        prev_per_tile = rows // ATTN_BLOCK

        def cur(head_offset):
            return pl.BlockSpec((1, 1, rows, width), lambda b, h, t, o=head_offset: (b, o + h, t, 0))

        def prev(head_offset):
            return pl.BlockSpec((1, 1, ATTN_BLOCK, width),
                                lambda b, h, t, o=head_offset, n=prev_per_tile: (b, o + h, jnp.maximum(n * t - 1, 0), 0))

        operands += [qv, kvv, kvv, kvv, kvv]
        in_specs += [cur(0), cur(0), prev(0), cur(N_HEADS), prev(N_HEADS)]

    return pl.pallas_call(
        _attn_kernel,
        grid=(b_, N_HEADS, n_tiles),
        in_specs=in_specs,
        out_specs=pl.BlockSpec((1, ATTN_TILE, hd), lambda b, h, t: (b, t, h)),
        out_shape=jax.ShapeDtypeStruct((b_, s, N_HEADS * hd), BF16),
        scratch_shapes=[pltpu.VMEM((N_GROUPS, ATTN_TILE, hd), F32), pltpu.VMEM((N_GROUPS, ATTN_TILE, hd), F32),
                        pltpu.VMEM((2, ATTN_BLOCK, 2 * ATTN_BLOCK), F32)],
        compiler_params=_params("arbitrary", "arbitrary", "arbitrary"),
        name="dilated_attention",
    )(*operands)


def kernel(x, p, positions, conv_w_in, conv_b_in, conv_dw, conv_dw_b, conv_ln_g, conv_ln_b, conv_w_out,
           kv_ln_g, kv_ln_b, w_kv, attn_w_q, attn_w_o, ln1_g, ln1_b, mlp_up, mlp_down, ln2_g, ln2_b,
           ple_proj, ple_gate):
    b_, s, d = x.shape
    t = b_ * s
    assert s % ATTN_TILE == 0 and d == D_MODEL

    def vec(v):
        return v.reshape(1, -1)

    def wb(w):
        return w.astype(BF16)

    cos, sin = _rope_tables(positions)

    pf = p.reshape(DEPTH, t, PLE_DIM)
    conv, (up, down, w_out) = _glu_conv(x, wb(conv_w_in[0]), vec(conv_b_in[0]), conv_dw[0],
                                        conv_dw_b[0].reshape(1, 1, d), [mlp_up, mlp_down, conv_w_out[0]])
    xf = x.reshape(t, d)
    xf = _proj_ln(conv.reshape(t, 1, d), xf, w_out, vec(conv_ln_g[0]), vec(conv_ln_b[0]),
                  vec(ln1_g[0]), vec(ln1_b[0]), conv_prologue=True, name="conv_out_ln1")
    xf, (gate, w_kv_b, w_q, w_o) = _mlp(xf, up, down, vec(ln2_g[0]), vec(ln2_b[0]), layer=0, name="mlp0",
                                        riders=[ple_gate, w_kv, attn_w_q[0], attn_w_o[0]])
    proj = wb(ple_proj)
    xf, x_plain, x_normed = _ple(xf, pf, proj, gate, layer=0, name="ple0", kv_ln=(vec(kv_ln_g), vec(kv_ln_b)))

    kv_views = _heads_proj(x_normed.reshape(b_, s, d), w_kv_b, cos, sin, n_out=2 * d, col_tile_offset=0,
                           dilations=DILATIONS, n_rotary_tiles=d // HEADS_TN, name="kv_proj")
    q_views = [
        _heads_proj(x_plain.reshape(b_, s, d), w_q, cos, sin, n_out=d, col_tile_offset=g * (d // HEADS_TN),
                    dilations=(dil,), n_rotary_tiles=None, name=f"q_proj_d{dil}")[0]
        for g, dil in enumerate(DILATIONS)]
    o = _attention(q_views, kv_views)
    xf = _proj_ln(o.reshape(t, d), xf, w_o, vec(ln1_g[1]), vec(ln1_b[1]), vec(ln1_g[1]), vec(ln1_b[1]),
                  conv_prologue=False, name="attn_out_ln1")
    xf, _ = _mlp(xf, up, down, vec(ln2_g[1]), vec(ln2_b[1]), layer=1, name="mlp1")
    xf, = _ple(xf, pf, proj, gate, layer=1, name="ple1")
    return xf.reshape(b_, s, d)
```

```python
import functools

import jax
import jax.numpy as jnp
from jax import lax
from jax.experimental import pallas as pl
from jax.experimental.pallas import tpu as pltpu

D_MODEL = 2048
DEPTH = 2
HEAD_DIM = 128
N_HEADS = D_MODEL // HEAD_DIM
DILATIONS = (1, 4, 16)
N_GROUPS = len(DILATIONS)
ATTN_BLOCK = 128
ATTN_TILE = ATTN_BLOCK * DILATIONS[-1]
CONV_WIDTH = 31
CONV_HALO = 32
D_FF = 4 * D_MODEL
PLE_DIM = 256
ROPE_THETA = 10000.0
LN_EPS = 1e-5
DEEPNORM_ALPHA = (2 * DEPTH) ** 0.25

VMEM_LIMIT_BYTES = 58 * 1024 * 1024
BF16_SUBLANES = 16
BF16 = jnp.bfloat16
F32 = jnp.float32


def _params(*semantics):
    return pltpu.CompilerParams(dimension_semantics=semantics, vmem_limit_bytes=VMEM_LIMIT_BYTES)


def _layer_norm(v, g, b):
    mu = jnp.mean(v, axis=-1, keepdims=True)
    c = v - mu
    var = jnp.mean(c * c, axis=-1, keepdims=True)
    return c * lax.rsqrt(var + LN_EPS) * g + b


def _mm(a, b):
    return jnp.dot(a, b, preferred_element_type=F32)


def _split(refs, *sizes):
    out, at = [], 0
    for n in sizes:
        out.append(refs[at:at + n])
        at += n
    assert at == len(refs)
    return out


class _Riders:
    def __init__(self, arrays, steps, step_of_grid):
        self.shapes = [a.shape for a in arrays]
        self.operands = [a.reshape(-1, a.shape[-1]) for a in arrays]
        self.specs, self.out_shapes = [], []
        for a in self.operands:
            rows, width = a.shape[0] // steps, a.shape[1]
            assert rows * steps == a.shape[0] and rows % BF16_SUBLANES == 0
            self.specs.append(pl.BlockSpec((rows, width), lambda *g: (step_of_grid(*g), 0)))
            self.out_shapes.append(jax.ShapeDtypeStruct(a.shape, BF16))

    def restore(self, rounded):
        return [r.reshape(s) for r, s in zip(rounded, self.shapes)]


def _round_riders(in_refs, out_refs):
    for src, dst in zip(in_refs, out_refs):
        dst[...] = src[...].astype(BF16)


CONV_TM = 256
CONV_TN = 1024
CONV_ROWS = 16


def _glu_conv_kernel(x_ref, wa_ref, wg_ref, ba_ref, bg_ref, dw_ref, dwb_ref, *refs, n_riders):
    rider_in, (o_ref,), rider_out, (ubuf,) = _split(refs, n_riders, 1, n_riders, 1)
    _round_riders(rider_in, rider_out)
    i = pl.program_id(2)

    @pl.when(i == 0)
    def _():
        ubuf[pl.ds(0, CONV_HALO)] = jnp.zeros((CONV_HALO, 1, CONV_TN), F32)

    xb = x_ref[0].astype(BF16)
    a = _mm(xb, wa_ref[...]) + ba_ref[...]
    g = _mm(xb, wg_ref[...]) + bg_ref[...]
    ubuf[pl.ds(CONV_HALO, CONV_TM), 0, :] = a * jax.nn.sigmoid(g)

    taps = [dw_ref[k, 0] for k in range(CONV_WIDTH)]
    bias = dwb_ref[0, 0]
    first = CONV_HALO - (CONV_WIDTH - 1)

    def body(c, carry):
        t0 = c * CONV_ROWS
        acc = [bias] * CONV_ROWS
        for e in range(CONV_ROWS + CONV_WIDTH - 1):
            row = ubuf[t0 + first + e, 0]
            for r in range(max(0, e - CONV_WIDTH + 1), min(CONV_ROWS, e + 1)):
                acc[r] = acc[r] + taps[e - r] * row
        for r in range(CONV_ROWS):
            o_ref[0, t0 + r, 0] = acc[r]
        return carry

    lax.fori_loop(0, CONV_TM // CONV_ROWS, body, 0, unroll=4)
    ubuf[pl.ds(0, CONV_HALO)] = ubuf[pl.ds(CONV_TM, CONV_HALO)]


def _glu_conv(x, w_in, b_in, dw, dw_b, riders):
    b_, s, d = x.shape
    nj = d // CONV_TN
    ni = s // CONV_TM
    rider = _Riders(riders, nj * b_ * ni, lambda j, b, i: (j * b_ + b) * ni + i)
    conv, *rounded = pl.pallas_call(
        functools.partial(_glu_conv_kernel, n_riders=len(riders)),
        grid=(nj, b_, ni),
        in_specs=[
            pl.BlockSpec((1, CONV_TM, d), lambda j, b, i: (b, i, 0)),
            pl.BlockSpec((d, CONV_TN), lambda j, b, i: (0, j)),
            pl.BlockSpec((d, CONV_TN), lambda j, b, i: (0, j + nj)),
            pl.BlockSpec((1, CONV_TN), lambda j, b, i: (0, j)),
            pl.BlockSpec((1, CONV_TN), lambda j, b, i: (0, j + nj)),
            pl.BlockSpec((CONV_WIDTH, 1, CONV_TN), lambda j, b, i: (0, 0, j)),
            pl.BlockSpec((1, 1, CONV_TN), lambda j, b, i: (0, 0, j)),
            *rider.specs,
        ],
        out_specs=[pl.BlockSpec((1, CONV_TM, 1, CONV_TN), lambda j, b, i: (b, i, 0, j)), *rider.specs],
        out_shape=[jax.ShapeDtypeStruct((b_, s, 1, d), F32), *rider.out_shapes],
        scratch_shapes=[pltpu.VMEM((CONV_HALO + CONV_TM, 1, CONV_TN), F32)],
        compiler_params=_params("arbitrary", "arbitrary", "arbitrary"),
        name="glu_conv",
    )(x, w_in, w_in, b_in, b_in, dw, dw_b, *rider.operands)
    return conv, rider.restore(rounded)


ROW_TM = 512
ROW_SPLIT = 2


def _proj_ln_kernel(act_ref, x_ref, w_ref, pg_ref, pb_ref, g_ref, b_ref, o_ref, *scratch, conv_prologue):
    for h in range(ROW_SPLIT):
        rows = pl.ds(h * (ROW_TM // ROW_SPLIT), ROW_TM // ROW_SPLIT)
        if conv_prologue:
            a2d_ref, = scratch
            a2d_ref[rows, :] = act_ref[rows, 0, :]
            a = _layer_norm(a2d_ref[rows, :], pg_ref[...], pb_ref[...])
            a = a * jax.nn.sigmoid(a)
        else:
            a = act_ref[rows, :]
        mix = _mm(a.astype(BF16), w_ref[...])
        o_ref[rows, :] = _layer_norm(DEEPNORM_ALPHA * x_ref[rows, :] + mix, g_ref[...], b_ref[...])


def _proj_ln(act, x, w, pro_g, pro_b, g, b, *, conv_prologue, name):
    t, d = x.shape
    row = pl.BlockSpec((ROW_TM, d), lambda i: (i, 0))
    vec = pl.BlockSpec((1, d), lambda i: (0, 0))
    act_spec = pl.BlockSpec((ROW_TM, 1, d), lambda i: (i, 0, 0)) if conv_prologue else row
    return pl.pallas_call(
        functools.partial(_proj_ln_kernel, conv_prologue=conv_prologue),
        grid=(t // ROW_TM,),
        in_specs=[act_spec, row, pl.BlockSpec((d, d), lambda i: (0, 0)), vec, vec, vec, vec],
        out_specs=row,
        out_shape=jax.ShapeDtypeStruct((t, d), F32),
        scratch_shapes=[pltpu.VMEM((ROW_TM, d), F32)] if conv_prologue else [],
        compiler_params=_params("arbitrary"),
        name=name,
    )(act, x, w, pro_g, pro_b, g, b)


MLP_TM = 1024
MLP_TF = 512
MLP_LAST_ROWS = 256


def _mlp_kernel(x_ref, up_ref, down_ref, g_ref, b_ref, *refs, n_riders):
    rider_in, (o_ref,), rider_out, (xb_ref,) = _split(refs, n_riders, 1, n_riders, 1)
    _round_riders(rider_in, rider_out)
    k = pl.program_id(1)

    def partial_out(rows):
        h = jnp.maximum(_mm(xb_ref[rows, :], up_ref[...]), 0.0)
        return _mm((h * h).astype(BF16), down_ref[...])

    last = pl.num_programs(1) - 1

    @pl.when(k == 0)
    def _():
        xb_ref[...] = x_ref[...].astype(BF16)
        o_ref[...] = partial_out(slice(None))

    @pl.when(jnp.logical_and(k > 0, k < last))
    def _():
        o_ref[...] += partial_out(slice(None))

    @pl.when(k == last)
    def _():
        for r0 in range(0, MLP_TM, MLP_LAST_ROWS):
            rows = pl.ds(r0, MLP_LAST_ROWS)
            y = o_ref[rows, :] + partial_out(rows)
            o_ref[rows, :] = _layer_norm(DEEPNORM_ALPHA * x_ref[rows, :] + y, g_ref[...], b_ref[...])


def _mlp(x, up, down, g, b, *, layer, name, riders=()):
    t, d = x.shape
    ff = up.shape[2]
    nk = ff // MLP_TF
    row = pl.BlockSpec((MLP_TM, d), lambda i, k: (i, 0))
    vec = pl.BlockSpec((1, d), lambda i, k: (0, 0))
    rider = _Riders(riders, (t // MLP_TM) * nk, lambda i, k: i * nk + k)
    out, *rounded = pl.pallas_call(
        functools.partial(_mlp_kernel, n_riders=len(riders)),
        grid=(t // MLP_TM, nk),
        in_specs=[row, pl.BlockSpec((None, d, MLP_TF), lambda i, k: (layer, 0, k)),
                  pl.BlockSpec((None, MLP_TF, d), lambda i, k: (layer, k, 0)), vec, vec, *rider.specs],
        out_specs=[row, *rider.specs],
        out_shape=[jax.ShapeDtypeStruct((t, d), F32), *rider.out_shapes],
        scratch_shapes=[pltpu.VMEM((MLP_TM, d), BF16)],
        compiler_params=_params("arbitrary", "arbitrary"),
        name=name,
    )(x, up, down, g, b, *rider.operands)
    return out, rider.restore(rounded)


def _ple_kernel(x_ref, p_ref, proj_ref, gate_ref, *refs, emit_mxu_inputs):
    if emit_mxu_inputs:
        g_ref, b_ref, o_ref, *mxu_inputs = refs
    else:
        (o_ref,), mxu_inputs = refs, ()
    for h in range(ROW_SPLIT):
        rows = pl.ds(h * (ROW_TM // ROW_SPLIT), ROW_TM // ROW_SPLIT)
        x = x_ref[rows, :]
        emb = _mm(p_ref[rows, :].astype(BF16), proj_ref[...])
        gate = jax.nn.sigmoid(_mm(x.astype(BF16), gate_ref[...]))
        out = x + emb * gate
        o_ref[rows, :] = out
        if mxu_inputs:
            plain_ref, normed_ref = mxu_inputs
            plain_ref[rows, :] = out.astype(BF16)
            normed_ref[rows, :] = _layer_norm(out, g_ref[...], b_ref[...]).astype(BF16)


def _ple(x, p, proj, gate, *, layer, name, kv_ln=None):
    t, d = x.shape
    row = pl.BlockSpec((ROW_TM, d), lambda i: (i, 0))
    vec = pl.BlockSpec((1, d), lambda i: (0, 0))
    emit = kv_ln is not None
    n_out = 3 if emit else 1
    return pl.pallas_call(
        functools.partial(_ple_kernel, emit_mxu_inputs=emit),
        grid=(t // ROW_TM,),
        in_specs=[row, pl.BlockSpec((None, ROW_TM, PLE_DIM), lambda i: (layer, i, 0)),
                  pl.BlockSpec((None, PLE_DIM, d), lambda i: (layer, 0, 0)),
                  pl.BlockSpec((None, d, d), lambda i: (layer, 0, 0))] + ([vec, vec] if emit else []),
        out_specs=[row] * n_out,
        out_shape=[jax.ShapeDtypeStruct((t, d), F32)] + [jax.ShapeDtypeStruct((t, d), BF16)] * (n_out - 1),
        compiler_params=_params("arbitrary"),
        name=name,
    )(x, p, proj, gate, *(kv_ln if emit else ()))


ROPE_TM = 512


def _rope_table_kernel(pos_ref, freq_ref, cos_ref, sin_ref):
    half_rows = ROPE_TM // 2
    low = lax.broadcasted_iota(jnp.int32, (half_rows, HEAD_DIM), 1) < HEAD_DIM // 2
    ang = jnp.where(low, pos_ref[:, 0:1], pos_ref[:, 1:2]) * freq_ref[...]
    c, s = jnp.cos(ang), jnp.sin(ang)
    c_sw, s_sw = pltpu.roll(c, HEAD_DIM // 2, axis=1), pltpu.roll(s, HEAD_DIM // 2, axis=1)
    even, odd = pl.ds(0, half_rows, stride=2), pl.ds(1, half_rows, stride=2)
    cos_ref[even, :] = jnp.where(low, c, c_sw)
    cos_ref[odd, :] = jnp.where(low, c_sw, c)
    sin_ref[even, :] = jnp.where(low, -s, s_sw)
    sin_ref[odd, :] = jnp.where(low, -s_sw, s)


def _rope_tables(positions):
    t = positions.size
    half = HEAD_DIM // 2
    inv_freq = ROPE_THETA ** (-jnp.arange(half, dtype=F32) * (2.0 / HEAD_DIM))
    freq = jnp.concatenate([inv_freq, inv_freq]).reshape(1, HEAD_DIM)
    pos = positions.astype(F32).reshape(t // 2, 2)
    tab = pl.BlockSpec((ROPE_TM, HEAD_DIM), lambda i: (i, 0))
    return pl.pallas_call(
        _rope_table_kernel,
        grid=(t // ROPE_TM,),
        in_specs=[pl.BlockSpec((ROPE_TM // 2, 2), lambda i: (i, 0)), pl.BlockSpec((1, HEAD_DIM), lambda i: (0, 0))],
        out_specs=[tab, tab],
        out_shape=[jax.ShapeDtypeStruct((t, HEAD_DIM), F32)] * 2,
        compiler_params=_params("arbitrary"),
        name="rope_tables",
    )(pos, freq)


def _rotary(v, cos, sin_signed):
    return v * cos + pltpu.roll(v, HEAD_DIM // 2, axis=1) * sin_signed


HEADS_TM = 1024
HEADS_TN = 1024
HEADS_PER_TILE = HEADS_TN // HEAD_DIM
HEADS_CHUNK = 256
HEADS_SUB_M = 256


def _heads_kernel(x_ref, w_ref, cos_ref, sin_ref, *refs, dilations, n_rotary_tiles):
    assert set(dilations) <= {1, 4, 16}
    outs = dict(zip(dilations, refs[:len(dilations)]))
    stage_ref, stage2_ref = refs[len(dilations):]
    quarter = HEADS_SUB_M // 4
    j = pl.program_id(2)

    cos, sin = cos_ref[...], sin_ref[...]
    if n_rotary_tiles is not None:
        rotate = j < n_rotary_tiles
        cos = jnp.where(rotate, cos, 1.0)
        sin = jnp.where(rotate, sin, 0.0)

    for c in range(HEADS_TN // HEADS_CHUNK):
        for m0 in range(0, HEADS_TM, HEADS_SUB_M):
            res = _mm(x_ref[0, pl.ds(m0, HEADS_SUB_M), :], w_ref[:, c * HEADS_CHUNK:(c + 1) * HEADS_CHUNK])
            for ch in range(HEADS_CHUNK // HEAD_DIM):
                hh = c * (HEADS_CHUNK // HEAD_DIM) + ch
                v = _rotary(res[:, ch * HEAD_DIM:(ch + 1) * HEAD_DIM],
                            cos[m0:m0 + HEADS_SUB_M], sin[m0:m0 + HEADS_SUB_M])
                if 1 in outs:
                    outs[1][0, hh, pl.ds(m0, HEADS_SUB_M), :] = v.astype(BF16)
                if max(dilations) == 1:
                    continue
                stage_ref[hh, pl.ds(m0, HEADS_SUB_M), :] = v
                for r4 in range(4):
                    part = stage_ref[hh, pl.ds(m0 + r4, quarter, stride=4), :]
                    if 4 in outs:
                        outs[4][0, hh, pl.ds(m0 // 4, quarter), r4 * HEAD_DIM:(r4 + 1) * HEAD_DIM] = part.astype(BF16)
                    if 16 not in outs:
                        continue
                    stage2_ref[hh, pl.ds(m0 + r4 * quarter, quarter), :] = part
                    for q4 in range(4):
                        r16 = 4 * q4 + r4
                        sub = stage2_ref[hh, pl.ds(m0 + r4 * quarter + q4, quarter // 4, stride=4), :]
                        outs[16][0, hh, pl.ds(m0 // 16, quarter // 4), r16 * HEAD_DIM:(r16 + 1) * HEAD_DIM] = (
                            sub.astype(BF16))


def _heads_proj(x, w, cos, sin, *, n_out, col_tile_offset, dilations, n_rotary_tiles, name):
    b_, s, d = x.shape
    per_b = s // HEADS_TM
    tab = pl.BlockSpec((HEADS_TM, HEAD_DIM), lambda bb, i, j: (bb * per_b + i, 0))
    n_heads = n_out // HEAD_DIM
    return pl.pallas_call(
        functools.partial(_heads_kernel, dilations=dilations, n_rotary_tiles=n_rotary_tiles),
        grid=(b_, per_b, n_out // HEADS_TN),
        in_specs=[pl.BlockSpec((1, HEADS_TM, d), lambda bb, i, j: (bb, i, 0)),
                  pl.BlockSpec((d, HEADS_TN), lambda bb, i, j: (0, col_tile_offset + j)), tab, tab],
        out_specs=[pl.BlockSpec((1, HEADS_PER_TILE, HEADS_TM // dil, dil * HEAD_DIM), lambda bb, i, j: (bb, j, i, 0))
                   for dil in dilations],
        out_shape=[jax.ShapeDtypeStruct((b_, n_heads, s // dil, dil * HEAD_DIM), BF16) for dil in dilations],
        scratch_shapes=[pltpu.VMEM((HEADS_PER_TILE, HEADS_TM, HEAD_DIM), F32)] * 2,
        compiler_params=_params("arbitrary", "arbitrary", "arbitrary"),
        name=name,
    )(x, w, cos, sin)


MERGE_ROWS = 256
LOG2_E = 1.4426950408889634


def _attn_kernel(*refs):
    group_refs = [refs[5 * g:5 * g + 5] for g in range(N_GROUPS)]
    o_ref, og_ref, lse_ref, bias_ref = refs[5 * N_GROUPS:]
    tile = pl.program_id(2)
    scale = HEAD_DIM ** -0.5
    nt_dims = (((1,), (1,)), ((), ()))

    qi = lax.broadcasted_iota(jnp.int32, (ATTN_BLOCK, 2 * ATTN_BLOCK), 0)
    kj = lax.broadcasted_iota(jnp.int32, (ATTN_BLOCK, 2 * ATTN_BLOCK), 1)
    bias = jnp.where(jnp.logical_and(kj >= qi, kj <= qi + ATTN_BLOCK), 0.0, -jnp.inf)
    bias_ref[0] = bias
    bias_ref[1] = jnp.where(jnp.logical_or(tile > 0, kj >= ATTN_BLOCK), bias, -jnp.inf)

    for g, (dil, (q_ref, kc_ref, kp_ref, vc_ref, vp_ref)) in enumerate(zip(DILATIONS, group_refs)):
        n_blocks = ATTN_TILE // (ATTN_BLOCK * dil)
        for r in range(dil):
            lanes = slice(r * HEAD_DIM, (r + 1) * HEAD_DIM)
            k_prev = kp_ref[0, 0, :, lanes]
            v_prev = vp_ref[0, 0, :, lanes]
            for n in range(n_blocks):
                rows = slice(n * ATTN_BLOCK, (n + 1) * ATTN_BLOCK)
                q = q_ref[0, 0, rows, lanes]
                k_cur = kc_ref[0, 0, rows, lanes]
                v_cur = vc_ref[0, 0, rows, lanes]
                keys = jnp.concatenate([k_prev, k_cur], axis=0)
                vals = jnp.concatenate([v_prev, v_cur], axis=0)
                s = lax.dot_general(q, keys, nt_dims, preferred_element_type=F32) + bias_ref[1 if n == 0 else 0]
                m = jnp.max(s, axis=1, keepdims=True)
                p = jnp.exp2((s - m) * (scale * LOG2_E))
                l = jnp.sum(p, axis=1, keepdims=True)
                pv = _mm(p.astype(BF16), vals)
                if dil == 1:
                    dst = pl.ds(n * ATTN_BLOCK, ATTN_BLOCK)
                else:
                    dst = pl.ds(n * ATTN_BLOCK * dil + r, ATTN_BLOCK, stride=dil)
                og_ref[g, dst, :] = pv / l
                lse_ref[g, dst, :] = jnp.broadcast_to(m * scale + jnp.log(l), (ATTN_BLOCK, HEAD_DIM))
                k_prev, v_prev = k_cur, v_cur

    for c in range(ATTN_TILE // MERGE_ROWS):
        rows = slice(c * MERGE_ROWS, (c + 1) * MERGE_ROWS)
        lses = [lse_ref[g, rows, :] for g in range(N_GROUPS)]
        top = functools.reduce(jnp.maximum, lses)
        wts = [jnp.exp(v - top) for v in lses]
        num = sum(w * og_ref[g, rows, :] for g, w in enumerate(wts))
        o_ref[0, rows, :] = (num / sum(wts)).astype(o_ref.dtype)


def _attention(q_views, kv_views):
    hd = HEAD_DIM
    b_ = q_views[0].shape[0]
    s = q_views[0].shape[2]
    n_tiles = s // ATTN_TILE
    operands, in_specs = [], []
    for dil, qv, kvv in zip(DILATIONS, q_views, kv_views):
        rows, width = ATTN_TILE // dil, dil * hd
        prev_per_tile = rows // ATTN_BLOCK

        def cur(head_offset):
            return pl.BlockSpec((1, 1, rows, width), lambda b, h, t, o=head_offset: (b, o + h, t, 0))

        def prev(head_offset):
            return pl.BlockSpec((1, 1, ATTN_BLOCK, width),
                                lambda b, h, t, o=head_offset, n=prev_per_tile: (b, o + h, jnp.maximum(n * t - 1, 0), 0))

        operands += [qv, kvv, kvv, kvv, kvv]
        in_specs += [cur(0), cur(0), prev(0), cur(N_HEADS), prev(N_HEADS)]

    return pl.pallas_call(
        _attn_kernel,
        grid=(b_, N_HEADS, n_tiles),
        in_specs=in_specs,
        out_specs=pl.BlockSpec((1, ATTN_TILE, hd), lambda b, h, t: (b, t, h)),
        out_shape=jax.ShapeDtypeStruct((b_, s, N_HEADS * hd), BF16),
        scratch_shapes=[pltpu.VMEM((N_GROUPS, ATTN_TILE, hd), F32), pltpu.VMEM((N_GROUPS, ATTN_TILE, hd), F32),
                        pltpu.VMEM((2, ATTN_BLOCK, 2 * ATTN_BLOCK), F32)],
        compiler_params=_params("arbitrary", "arbitrary", "arbitrary"),
        name="dilated_attention",
    )(*operands)


def kernel(x, p, positions, conv_w_in, conv_b_in, conv_dw, conv_dw_b, conv_ln_g, conv_ln_b, conv_w_out,
           kv_ln_g, kv_ln_b, w_kv, attn_w_q, attn_w_o, ln1_g, ln1_b, mlp_up, mlp_down, ln2_g, ln2_b,
           ple_proj, ple_gate):
    b_, s, d = x.shape
    t = b_ * s
    assert s % ATTN_TILE == 0 and d == D_MODEL

    def vec(v):
        return v.reshape(1, -1)

    def wb(w):
        return w.astype(BF16)

    cos, sin = _rope_tables(positions)

    pf = p.reshape(DEPTH, t, PLE_DIM)
    conv, (up, down, w_out) = _glu_conv(x, wb(conv_w_in[0]), vec(conv_b_in[0]), conv_dw[0],
                                        conv_dw_b[0].reshape(1, 1, d), [mlp_up, mlp_down, conv_w_out[0]])
    xf = x.reshape(t, d)
    xf = _proj_ln(conv.reshape(t, 1, d), xf, w_out, vec(conv_ln_g[0]), vec(conv_ln_b[0]),
                  vec(ln1_g[0]), vec(ln1_b[0]), conv_prologue=True, name="conv_out_ln1")
    xf, (gate, w_kv_b, w_q, w_o) = _mlp(xf, up, down, vec(ln2_g[0]), vec(ln2_b[0]), layer=0, name="mlp0",
                                        riders=[ple_gate, w_kv, attn_w_q[0], attn_w_o[0]])
    proj = wb(ple_proj)
    xf, x_plain, x_normed = _ple(xf, pf, proj, gate, layer=0, name="ple0", kv_ln=(vec(kv_ln_g), vec(kv_ln_b)))

    kv_views = _heads_proj(x_normed.reshape(b_, s, d), w_kv_b, cos, sin, n_out=2 * d, col_tile_offset=0,
                           dilations=DILATIONS, n_rotary_tiles=d // HEADS_TN, name="kv_proj")
    q_views = [
        _heads_proj(x_plain.reshape(b_, s, d), w_q, cos, sin, n_out=d, col_tile_offset=g * (d // HEADS_TN),
                    dilations=(dil,), n_rotary_tiles=None, name=f"q_proj_d{dil}")[0]
        for g, dil in enumerate(DILATIONS)]
    o = _attention(q_views, kv_views)
    xf = _proj_ln(o.reshape(t, d), xf, w_o, vec(ln1_g[1]), vec(ln1_b[1]), vec(ln1_g[1]), vec(ln1_b[1]),
                  conv_prologue=False, name="attn_out_ln1")
    xf, _ = _mlp(xf, up, down, vec(ln2_g[1]), vec(ln2_b[1]), layer=1, name="mlp1")
    xf, = _ple(xf, pf, proj, gate, layer=1, name="ple1")
    return xf.reshape(b_, s, d)
```

```python
import functools

import jax
import jax.numpy as jnp
from jax import lax
from jax.experimental import pallas as pl
from jax.experimental.pallas import tpu as pltpu

D_MODEL = 2048
DEPTH = 2
HEAD_DIM = 128
N_HEADS = D_MODEL // HEAD_DIM
DILATIONS = (1, 4, 16)
N_GROUPS = len(DILATIONS)
ATTN_BLOCK = 128
ATTN_TILE = ATTN_BLOCK * DILATIONS[-1]
CONV_WIDTH = 31
CONV_HALO = 32
D_FF = 4 * D_MODEL
PLE_DIM = 256
ROPE_THETA = 10000.0
LN_EPS = 1e-5
DEEPNORM_ALPHA = (2 * DEPTH) ** 0.25

VMEM_LIMIT_BYTES = 58 * 1024 * 1024
BF16_SUBLANES = 16
BF16 = jnp.bfloat16
F32 = jnp.float32


def _params(*semantics):
    return pltpu.CompilerParams(dimension_semantics=semantics, vmem_limit_bytes=VMEM_LIMIT_BYTES)


def _layer_norm(v, g, b):
    mu = jnp.mean(v, axis=-1, keepdims=True)
    c = v - mu
    var = jnp.mean(c * c, axis=-1, keepdims=True)
    return c * lax.rsqrt(var + LN_EPS) * g + b


def _mm(a, b):
    return jnp.dot(a, b, preferred_element_type=F32)


def _split(refs, *sizes):
    out, at = [], 0
    for n in sizes:
        out.append(refs[at:at + n])
        at += n
    assert at == len(refs)
    return out


class _Riders:
    def __init__(self, arrays, steps, step_of_grid):
        self.shapes = [a.shape for a in arrays]
        self.operands = [a.reshape(-1, a.shape[-1]) for a in arrays]
        self.specs, self.out_shapes = [], []
        for a in self.operands:
            rows, width = a.shape[0] // steps, a.shape[1]
            assert rows * steps == a.shape[0] and rows % BF16_SUBLANES == 0
            self.specs.append(pl.BlockSpec((rows, width), lambda *g: (step_of_grid(*g), 0)))
            self.out_shapes.append(jax.ShapeDtypeStruct(a.shape, BF16))

    def restore(self, rounded):
        return [r.reshape(s) for r, s in zip(rounded, self.shapes)]


def _round_riders(in_refs, out_refs):
    for src, dst in zip(in_refs, out_refs):
        dst[...] = src[...].astype(BF16)


CONV_TM = 256
CONV_TN = 1024
CONV_ROWS = 16


def _glu_conv_kernel(x_ref, wa_ref, wg_ref, ba_ref, bg_ref, dw_ref, dwb_ref, *refs, n_riders):
    rider_in, (o_ref,), rider_out, (ubuf,) = _split(refs, n_riders, 1, n_riders, 1)
    _round_riders(rider_in, rider_out)
    i = pl.program_id(2)

    @pl.when(i == 0)
    def _():
        ubuf[pl.ds(0, CONV_HALO)] = jnp.zeros((CONV_HALO, 1, CONV_TN), F32)

    xb = x_ref[0].astype(BF16)
    a = _mm(xb, wa_ref[...]) + ba_ref[...]
    g = _mm(xb, wg_ref[...]) + bg_ref[...]
    ubuf[pl.ds(CONV_HALO, CONV_TM), 0, :] = a * jax.nn.sigmoid(g)

    taps = [dw_ref[k, 0] for k in range(CONV_WIDTH)]
    bias = dwb_ref[0, 0]
    first = CONV_HALO - (CONV_WIDTH - 1)

    def body(c, carry):
        t0 = c * CONV_ROWS
        acc = [bias] * CONV_ROWS
        for e in range(CONV_ROWS + CONV_WIDTH - 1):
            row = ubuf[t0 + first + e, 0]
            for r in range(max(0, e - CONV_WIDTH + 1), min(CONV_ROWS, e + 1)):
                acc[r] = acc[r] + taps[e - r] * row
        for r in range(CONV_ROWS):
            o_ref[0, t0 + r, :] = acc[r]
        return carry

    lax.fori_loop(0, CONV_TM // CONV_ROWS, body, 0, unroll=4)
    ubuf[pl.ds(0, CONV_HALO)] = ubuf[pl.ds(CONV_TM, CONV_HALO)]


def _glu_conv(x, w_in, b_in, dw, dw_b, riders):
    b_, s, d = x.shape
    nj = d // CONV_TN
    ni = s // CONV_TM
    rider = _Riders(riders, nj * b_ * ni, lambda j, b, i: (j * b_ + b) * ni + i)
    conv, *rounded = pl.pallas_call(
        functools.partial(_glu_conv_kernel, n_riders=len(riders)),
        grid=(nj, b_, ni),
        in_specs=[
            pl.BlockSpec((1, CONV_TM, d), lambda j, b, i: (b, i, 0)),
            pl.BlockSpec((d, CONV_TN), lambda j, b, i: (0, j)),
            pl.BlockSpec((d, CONV_TN), lambda j, b, i: (0, j + nj)),
            pl.BlockSpec((1, CONV_TN), lambda j, b, i: (0, j)),
            pl.BlockSpec((1, CONV_TN), lambda j, b, i: (0, j + nj)),
            pl.BlockSpec((CONV_WIDTH, 1, CONV_TN), lambda j, b, i: (0, 0, j)),
            pl.BlockSpec((1, 1, CONV_TN), lambda j, b, i: (0, 0, j)),
            *rider.specs,
        ],
        out_specs=[pl.BlockSpec((1, CONV_TM, CONV_TN), lambda j, b, i: (b, i, j)), *rider.specs],
        out_shape=[jax.ShapeDtypeStruct((b_, s, d), F32), *rider.out_shapes],
        scratch_shapes=[pltpu.VMEM((CONV_HALO + CONV_TM, 1, CONV_TN), F32)],
        compiler_params=_params("arbitrary", "arbitrary", "arbitrary"),
        name="glu_conv",
    )(x, w_in, w_in, b_in, b_in, dw, dw_b, *rider.operands)
    return conv, rider.restore(rounded)


ROW_TM = 512
ROW_SPLIT = 2


def _proj_ln_kernel(act_ref, x_ref, w_ref, pg_ref, pb_ref, g_ref, b_ref, o_ref, *, conv_prologue):
    for h in range(ROW_SPLIT):
        rows = pl.ds(h * (ROW_TM // ROW_SPLIT), ROW_TM // ROW_SPLIT)
        a = act_ref[rows, :]
        if conv_prologue:
            a = _layer_norm(a, pg_ref[...], pb_ref[...])
            a = a * jax.nn.sigmoid(a)
        mix = _mm(a.astype(BF16), w_ref[...])
        o_ref[rows, :] = _layer_norm(DEEPNORM_ALPHA * x_ref[rows, :] + mix, g_ref[...], b_ref[...])


def _proj_ln(act, x, w, pro_g, pro_b, g, b, *, conv_prologue, name):
    t, d = x.shape
    row = pl.BlockSpec((ROW_TM, d), lambda i: (i, 0))
    vec = pl.BlockSpec((1, d), lambda i: (0, 0))
    return pl.pallas_call(
        functools.partial(_proj_ln_kernel, conv_prologue=conv_prologue),
        grid=(t // ROW_TM,),
        in_specs=[row, row, pl.BlockSpec((d, d), lambda i: (0, 0)), vec, vec, vec, vec],
        out_specs=row,
        out_shape=jax.ShapeDtypeStruct((t, d), F32),
        compiler_params=_params("arbitrary"),
        name=name,
    )(act, x, w, pro_g, pro_b, g, b)


MLP_TM = 1024
MLP_TF = 512
MLP_LAST_ROWS = 256


def _mlp_kernel(x_ref, up_ref, down_ref, g_ref, b_ref, *refs, n_riders):
    rider_in, (o_ref,), rider_out, (xb_ref,) = _split(refs, n_riders, 1, n_riders, 1)
    _round_riders(rider_in, rider_out)
    k = pl.program_id(1)

    def partial_out(rows):
        h = jnp.maximum(_mm(xb_ref[rows, :], up_ref[...]), 0.0)
        return _mm((h * h).astype(BF16), down_ref[...])

    last = pl.num_programs(1) - 1

    @pl.when(k == 0)
    def _():
        xb_ref[...] = x_ref[...].astype(BF16)
        o_ref[...] = partial_out(slice(None))

    @pl.when(jnp.logical_and(k > 0, k < last))
    def _():
        o_ref[...] += partial_out(slice(None))

    @pl.when(k == last)
    def _():
        for r0 in range(0, MLP_TM, MLP_LAST_ROWS):
            rows = pl.ds(r0, MLP_LAST_ROWS)
            y = o_ref[rows, :] + partial_out(rows)
            o_ref[rows, :] = _layer_norm(DEEPNORM_ALPHA * x_ref[rows, :] + y, g_ref[...], b_ref[...])


def _mlp(x, up, down, g, b, *, layer, name, riders=()):
    t, d = x.shape
    ff = up.shape[2]
    nk = ff // MLP_TF
    row = pl.BlockSpec((MLP_TM, d), lambda i, k: (i, 0))
    vec = pl.BlockSpec((1, d), lambda i, k: (0, 0))
    rider = _Riders(riders, (t // MLP_TM) * nk, lambda i, k: i * nk + k)
    out, *rounded = pl.pallas_call(
        functools.partial(_mlp_kernel, n_riders=len(riders)),
        grid=(t // MLP_TM, nk),
        in_specs=[row, pl.BlockSpec((None, d, MLP_TF), lambda i, k: (layer, 0, k)),
                  pl.BlockSpec((None, MLP_TF, d), lambda i, k: (layer, k, 0)), vec, vec, *rider.specs],
        out_specs=[row, *rider.specs],
        out_shape=[jax.ShapeDtypeStruct((t, d), F32), *rider.out_shapes],
        scratch_shapes=[pltpu.VMEM((MLP_TM, d), BF16)],
        compiler_params=_params("arbitrary", "arbitrary"),
        name=name,
    )(x, up, down, g, b, *rider.operands)
    return out, rider.restore(rounded)


def _ple_kernel(x_ref, p_ref, proj_ref, gate_ref, *refs, emit_mxu_inputs):
    if emit_mxu_inputs:
        g_ref, b_ref, o_ref, *mxu_inputs = refs
    else:
        (o_ref,), mxu_inputs = refs, ()
    for h in range(ROW_SPLIT):
        rows = pl.ds(h * (ROW_TM // ROW_SPLIT), ROW_TM // ROW_SPLIT)
        x = x_ref[rows, :]
        emb = _mm(p_ref[rows, :].astype(BF16), proj_ref[...])
        gate = jax.nn.sigmoid(_mm(x.astype(BF16), gate_ref[...]))
        out = x + emb * gate
        o_ref[rows, :] = out
        if mxu_inputs:
            plain_ref, normed_ref = mxu_inputs
            plain_ref[rows, :] = out.astype(BF16)
            normed_ref[rows, :] = _layer_norm(out, g_ref[...], b_ref[...]).astype(BF16)


def _ple(x, p, proj, gate, *, layer, name, kv_ln=None):
    t, d = x.shape
    row = pl.BlockSpec((ROW_TM, d), lambda i: (i, 0))
    vec = pl.BlockSpec((1, d), lambda i: (0, 0))
    emit = kv_ln is not None
    n_out = 3 if emit else 1
    return pl.pallas_call(
        functools.partial(_ple_kernel, emit_mxu_inputs=emit),
        grid=(t // ROW_TM,),
        in_specs=[row, pl.BlockSpec((None, ROW_TM, PLE_DIM), lambda i: (layer, i, 0)),
                  pl.BlockSpec((None, PLE_DIM, d), lambda i: (layer, 0, 0)),
                  pl.BlockSpec((None, d, d), lambda i: (layer, 0, 0))] + ([vec, vec] if emit else []),
        out_specs=[row] * n_out,
        out_shape=[jax.ShapeDtypeStruct((t, d), F32)] + [jax.ShapeDtypeStruct((t, d), BF16)] * (n_out - 1),
        compiler_params=_params("arbitrary"),
        name=name,
    )(x, p, proj, gate, *(kv_ln if emit else ()))


ROPE_TM = 512


def _rope_table_kernel(pos_ref, freq_ref, cos_ref, sin_ref):
    half_rows = ROPE_TM // 2
    low = lax.broadcasted_iota(jnp.int32, (half_rows, HEAD_DIM), 1) < HEAD_DIM // 2
    ang = jnp.where(low, pos_ref[:, 0:1], pos_ref[:, 1:2]) * freq_ref[...]
    c, s = jnp.cos(ang), jnp.sin(ang)
    c_sw, s_sw = pltpu.roll(c, HEAD_DIM // 2, axis=1), pltpu.roll(s, HEAD_DIM // 2, axis=1)
    even, odd = pl.ds(0, half_rows, stride=2), pl.ds(1, half_rows, stride=2)
    cos_ref[even, :] = jnp.where(low, c, c_sw)
    cos_ref[odd, :] = jnp.where(low, c_sw, c)
    sin_ref[even, :] = jnp.where(low, -s, s_sw)
    sin_ref[odd, :] = jnp.where(low, -s_sw, s)


def _rope_tables(positions):
    t = positions.size
    half = HEAD_DIM // 2
    inv_freq = ROPE_THETA ** (-jnp.arange(half, dtype=F32) * (2.0 / HEAD_DIM))
    freq = jnp.concatenate([inv_freq, inv_freq]).reshape(1, HEAD_DIM)
    pos = positions.astype(F32).reshape(t // 2, 2)
    tab = pl.BlockSpec((ROPE_TM, HEAD_DIM), lambda i: (i, 0))
    return pl.pallas_call(
        _rope_table_kernel,
        grid=(t // ROPE_TM,),
        in_specs=[pl.BlockSpec((ROPE_TM // 2, 2), lambda i: (i, 0)), pl.BlockSpec((1, HEAD_DIM), lambda i: (0, 0))],
        out_specs=[tab, tab],
        out_shape=[jax.ShapeDtypeStruct((t, HEAD_DIM), F32)] * 2,
        compiler_params=_params("arbitrary"),
        name="rope_tables",
    )(pos, freq)


def _rotary(v, cos, sin_signed):
    return v * cos + pltpu.roll(v, HEAD_DIM // 2, axis=1) * sin_signed


HEADS_TM = 1024
HEADS_TN = 1024
HEADS_PER_TILE = HEADS_TN // HEAD_DIM
HEADS_CHUNK = 256
HEADS_SUB_M = 256


def _heads_kernel(x_ref, w_ref, cos_ref, sin_ref, *refs, dilations, n_rotary_tiles):
    assert set(dilations) <= {1, 4, 16}
    outs = dict(zip(dilations, refs[:len(dilations)]))
    stage_ref, stage2_ref = refs[len(dilations):]
    quarter = HEADS_SUB_M // 4
    j = pl.program_id(2)

    cos, sin = cos_ref[...], sin_ref[...]
    if n_rotary_tiles is not None:
        rotate = j < n_rotary_tiles
        cos = jnp.where(rotate, cos, 1.0)
        sin = jnp.where(rotate, sin, 0.0)

    for c in range(HEADS_TN // HEADS_CHUNK):
        for m0 in range(0, HEADS_TM, HEADS_SUB_M):
            res = _mm(x_ref[0, pl.ds(m0, HEADS_SUB_M), :], w_ref[:, c * HEADS_CHUNK:(c + 1) * HEADS_CHUNK])
            for ch in range(HEADS_CHUNK // HEAD_DIM):
                hh = c * (HEADS_CHUNK // HEAD_DIM) + ch
                v = _rotary(res[:, ch * HEAD_DIM:(ch + 1) * HEAD_DIM],
                            cos[m0:m0 + HEADS_SUB_M], sin[m0:m0 + HEADS_SUB_M])
                if 1 in outs:
                    outs[1][0, hh, pl.ds(m0, HEADS_SUB_M), :] = v.astype(BF16)
                if max(dilations) == 1:
                    continue
                stage_ref[hh, pl.ds(m0, HEADS_SUB_M), :] = v
                for r4 in range(4):
                    part = stage_ref[hh, pl.ds(m0 + r4, quarter, stride=4), :]
                    if 4 in outs:
                        outs[4][0, hh, pl.ds(m0 // 4, quarter), r4 * HEAD_DIM:(r4 + 1) * HEAD_DIM] = part.astype(BF16)
                    if 16 not in outs:
                        continue
                    stage2_ref[hh, pl.ds(m0 + r4 * quarter, quarter), :] = part
                    for q4 in range(4):
                        r16 = 4 * q4 + r4
                        sub = stage2_ref[hh, pl.ds(m0 + r4 * quarter + q4, quarter // 4, stride=4), :]
                        outs[16][0, hh, pl.ds(m0 // 16, quarter // 4), r16 * HEAD_DIM:(r16 + 1) * HEAD_DIM] = (
                            sub.astype(BF16))


def _heads_proj(x, w, cos, sin, *, n_out, col_tile_offset, dilations, n_rotary_tiles, name):
    b_, s, d = x.shape
    per_b = s // HEADS_TM
    tab = pl.BlockSpec((HEADS_TM, HEAD_DIM), lambda bb, i, j: (bb * per_b + i, 0))
    n_heads = n_out // HEAD_DIM
    return pl.pallas_call(
        functools.partial(_heads_kernel, dilations=dilations, n_rotary_tiles=n_rotary_tiles),
        grid=(b_, per_b, n_out // HEADS_TN),
        in_specs=[pl.BlockSpec((1, HEADS_TM, d), lambda bb, i, j: (bb, i, 0)),
                  pl.BlockSpec((d, HEADS_TN), lambda bb, i, j: (0, col_tile_offset + j)), tab, tab],
        out_specs=[pl.BlockSpec((1, HEADS_PER_TILE, HEADS_TM // dil, dil * HEAD_DIM), lambda bb, i, j: (bb, j, i, 0))
                   for dil in dilations],
        out_shape=[jax.ShapeDtypeStruct((b_, n_heads, s // dil, dil * HEAD_DIM), BF16) for dil in dilations],
        scratch_shapes=[pltpu.VMEM((HEADS_PER_TILE, HEADS_TM, HEAD_DIM), F32)] * 2,
        compiler_params=_params("arbitrary", "arbitrary", "arbitrary"),
        name=name,
    )(x, w, cos, sin)


MERGE_ROWS = 256
LOG2_E = 1.4426950408889634


def _attn_kernel(*refs):
    group_refs = [refs[5 * g:5 * g + 5] for g in range(N_GROUPS)]
    o_ref, og_ref, lse_ref, bias_ref = refs[5 * N_GROUPS:]
    tile = pl.program_id(2)
    scale = HEAD_DIM ** -0.5
    nt_dims = (((1,), (1,)), ((), ()))

    qi = lax.broadcasted_iota(jnp.int32, (ATTN_BLOCK, 2 * ATTN_BLOCK), 0)
    kj = lax.broadcasted_iota(jnp.int32, (ATTN_BLOCK, 2 * ATTN_BLOCK), 1)
    bias = jnp.where(jnp.logical_and(kj >= qi, kj <= qi + ATTN_BLOCK), 0.0, -jnp.inf)
    bias_ref[0] = bias
    bias_ref[1] = jnp.where(jnp.logical_or(tile > 0, kj >= ATTN_BLOCK), bias, -jnp.inf)

    for g, (dil, (q_ref, kc_ref, kp_ref, vc_ref, vp_ref)) in enumerate(zip(DILATIONS, group_refs)):
        n_blocks = ATTN_TILE // (ATTN_BLOCK * dil)
        for r in range(dil):
            lanes = slice(r * HEAD_DIM, (r + 1) * HEAD_DIM)
            k_prev = kp_ref[0, 0, :, lanes]
            v_prev = vp_ref[0, 0, :, lanes]
            for n in range(n_blocks):
                rows = slice(n * ATTN_BLOCK, (n + 1) * ATTN_BLOCK)
                q = q_ref[0, 0, rows, lanes]
                k_cur = kc_ref[0, 0, rows, lanes]
                v_cur = vc_ref[0, 0, rows, lanes]
                keys = jnp.concatenate([k_prev, k_cur], axis=0)
                vals = jnp.concatenate([v_prev, v_cur], axis=0)
                s = lax.dot_general(q, keys, nt_dims, preferred_element_type=F32) + bias_ref[1 if n == 0 else 0]
                m = jnp.max(s, axis=1, keepdims=True)
                p = jnp.exp2((s - m) * (scale * LOG2_E))
                l = jnp.sum(p, axis=1, keepdims=True)
                pv = _mm(p.astype(BF16), vals)
                if dil == 1:
                    dst = pl.ds(n * ATTN_BLOCK, ATTN_BLOCK)
                else:
                    dst = pl.ds(n * ATTN_BLOCK * dil + r, ATTN_BLOCK, stride=dil)
                og_ref[g, dst, :] = pv / l
                lse_ref[g, dst, :] = jnp.broadcast_to(m * scale + jnp.log(l), (ATTN_BLOCK, HEAD_DIM))
                k_prev, v_prev = k_cur, v_cur

    for c in range(ATTN_TILE // MERGE_ROWS):
        rows = slice(c * MERGE_ROWS, (c + 1) * MERGE_ROWS)
        lses = [lse_ref[g, rows, :] for g in range(N_GROUPS)]
        top = functools.reduce(jnp.maximum, lses)
        wts = [jnp.exp(v - top) for v in lses]
        num = sum(w * og_ref[g, rows, :] for g, w in enumerate(wts))
        o_ref[0, rows, :] = (num / sum(wts)).astype(o_ref.dtype)


def _attention(q_views, kv_views):
    hd = HEAD_DIM
    b_ = q_views[0].shape[0]
    s = q_views[0].shape[2]
    n_tiles = s // ATTN_TILE
    operands, in_specs = [], []
    for dil, qv, kvv in zip(DILATIONS, q_views, kv_views):
        rows, width = ATTN_TILE // dil, dil * hd
        prev_per_tile = rows // ATTN_BLOCK

        def cur(head_offset):
            return pl.BlockSpec((1, 1, rows, width), lambda b, h, t, o=head_offset: (b, o + h, t, 0))

        def prev(head_offset):
            return pl.BlockSpec((1, 1, ATTN_BLOCK, width),
                                lambda b, h, t, o=head_offset, n=prev_per_tile: (b, o + h, jnp.maximum(n * t - 1, 0), 0))

        operands += [qv, kvv, kvv, kvv, kvv]
        in_specs += [cur(0), cur(0), prev(0), cur(N_HEADS), prev(N_HEADS)]

    return pl.pallas_call(
        _attn_kernel,
        grid=(b_, N_HEADS, n_tiles),
        in_specs=in_specs,
        out_specs=pl.BlockSpec((1, ATTN_TILE, hd), lambda b, h, t: (b, t, h)),
        out_shape=jax.ShapeDtypeStruct((b_, s, N_HEADS * hd), BF16),
        scratch_shapes=[pltpu.VMEM((N_GROUPS, ATTN_TILE, hd), F32), pltpu.VMEM((N_GROUPS, ATTN_TILE, hd), F32),
                        pltpu.VMEM((2, ATTN_BLOCK, 2 * ATTN_BLOCK), F32)],
        compiler_params=_params("arbitrary", "arbitrary", "arbitrary"),
        name="dilated_attention",
    )(*operands)


def kernel(x, p, positions, conv_w_in, conv_b_in, conv_dw, conv_dw_b, conv_ln_g, conv_ln_b, conv_w_out,
           kv_ln_g, kv_ln_b, w_kv, attn_w_q, attn_w_o, ln1_g, ln1_b, mlp_up, mlp_down, ln2_g, ln2_b,
           ple_proj, ple_gate):
    b_, s, d = x.shape
    t = b_ * s
    assert s % ATTN_TILE == 0 and d == D_MODEL

    def vec(v):
        return v.reshape(1, -1)

    def wb(w):
        return w.astype(BF16)

    cos, sin = _rope_tables(positions)

    pf = p.reshape(DEPTH, t, PLE_DIM)
    conv, (up, down, w_out) = _glu_conv(x, wb(conv_w_in[0]), vec(conv_b_in[0]), conv_dw[0],
                                        conv_dw_b[0].reshape(1, 1, d), [mlp_up, mlp_down, conv_w_out[0]])
    xf = x.reshape(t, d)
    xf = _proj_ln(conv.reshape(t, d), xf, w_out, vec(conv_ln_g[0]), vec(conv_ln_b[0]),
                  vec(ln1_g[0]), vec(ln1_b[0]), conv_prologue=True, name="conv_out_ln1")
    xf, (gate, w_kv_b, w_q, w_o) = _mlp(xf, up, down, vec(ln2_g[0]), vec(ln2_b[0]), layer=0, name="mlp0",
                                        riders=[ple_gate, w_kv, attn_w_q[0], attn_w_o[0]])
    proj = wb(ple_proj)
    xf, x_plain, x_normed = _ple(xf, pf, proj, gate, layer=0, name="ple0", kv_ln=(vec(kv_ln_g), vec(kv_ln_b)))

    kv_views = _heads_proj(x_normed.reshape(b_, s, d), w_kv_b, cos, sin, n_out=2 * d, col_tile_offset=0,
                           dilations=DILATIONS, n_rotary_tiles=d // HEADS_TN, name="kv_proj")
    q_views = [
        _heads_proj(x_plain.reshape(b_, s, d), w_q, cos, sin, n_out=d, col_tile_offset=g * (d // HEADS_TN),
                    dilations=(dil,), n_rotary_tiles=None, name=f"q_proj_d{dil}")[0]
        for g, dil in enumerate(DILATIONS)]
    o = _attention(q_views, kv_views)
    xf = _proj_ln(o.reshape(t, d), xf, w_o, vec(ln1_g[1]), vec(ln1_b[1]), vec(ln1_g[1]), vec(ln1_b[1]),
                  conv_prologue=False, name="attn_out_ln1")
    xf, _ = _mlp(xf, up, down, vec(ln2_g[1]), vec(ln2_b[1]), layer=1, name="mlp1")
    xf, = _ple(xf, pf, proj, gate, layer=1, name="ple1")
    return xf.reshape(b_, s, d)
```

```python
import functools

import jax
import jax.numpy as jnp
from jax import lax
from jax.experimental import pallas as pl
from jax.experimental.pallas import tpu as pltpu

D_MODEL = 2048
DEPTH = 2
HEAD_DIM = 128
N_HEADS = D_MODEL // HEAD_DIM
DILATIONS = (1, 4, 16)
N_GROUPS = len(DILATIONS)
ATTN_BLOCK = 128
ATTN_TILE = ATTN_BLOCK * DILATIONS[-1]
CONV_WIDTH = 31
CONV_HALO = 32
D_FF = 4 * D_MODEL
PLE_DIM = 256
ROPE_THETA = 10000.0
LN_EPS = 1e-5
DEEPNORM_ALPHA = (2 * DEPTH) ** 0.25

VMEM_LIMIT_BYTES = 58 * 1024 * 1024
BF16_SUBLANES = 16
BF16 = jnp.bfloat16
F32 = jnp.float32


def _params(*semantics):
    return pltpu.CompilerParams(dimension_semantics=semantics, vmem_limit_bytes=VMEM_LIMIT_BYTES)


def _layer_norm(v, g, b):
    mu = jnp.mean(v, axis=-1, keepdims=True)
    c = v - mu
    var = jnp.mean(c * c, axis=-1, keepdims=True)
    return c * lax.rsqrt(var + LN_EPS) * g + b


def _mm(a, b):
    return jnp.dot(a, b, preferred_element_type=F32)


def _split(refs, *sizes):
    out, at = [], 0
    for n in sizes:
        out.append(refs[at:at + n])
        at += n
    assert at == len(refs)
    return out


class _Riders:
    def __init__(self, arrays, steps, step_of_grid):
        self.shapes = [a.shape for a in arrays]
        self.operands = [a.reshape(-1, a.shape[-1]) for a in arrays]
        self.specs, self.out_shapes = [], []
        for a in self.operands:
            rows, width = a.shape[0] // steps, a.shape[1]
            assert rows * steps == a.shape[0] and rows % BF16_SUBLANES == 0
            self.specs.append(pl.BlockSpec((rows, width), lambda *g: (step_of_grid(*g), 0)))
            self.out_shapes.append(jax.ShapeDtypeStruct(a.shape, BF16))

    def restore(self, rounded):
        return [r.reshape(s) for r, s in zip(rounded, self.shapes)]


def _round_riders(in_refs, out_refs):
    for src, dst in zip(in_refs, out_refs):
        dst[...] = src[...].astype(BF16)


CONV_TM = 256
CONV_TN = 1024
CONV_ROWS = 16


def _glu_conv_kernel(x_ref, wa_ref, wg_ref, ba_ref, bg_ref, dw_ref, dwb_ref, *refs, n_riders):
    rider_in, (o_ref,), rider_out, (ubuf,) = _split(refs, n_riders, 1, n_riders, 1)
    _round_riders(rider_in, rider_out)
    i = pl.program_id(2)

    @pl.when(i == 0)
    def _():
        ubuf[pl.ds(0, CONV_HALO)] = jnp.zeros((CONV_HALO, 1, CONV_TN), F32)

    xb = x_ref[0].astype(BF16)
    a = _mm(xb, wa_ref[...]) + ba_ref[...]
    g = _mm(xb, wg_ref[...]) + bg_ref[...]
    ubuf[pl.ds(CONV_HALO, CONV_TM), 0, :] = a * jax.nn.sigmoid(g)

    taps = [dw_ref[k, 0] for k in range(CONV_WIDTH)]
    bias = dwb_ref[0, 0]
    first = CONV_HALO - (CONV_WIDTH - 1)

    def body(c, carry):
        t0 = c * CONV_ROWS
        acc = [bias] * CONV_ROWS
        for e in range(CONV_ROWS + CONV_WIDTH - 1):
            row = ubuf[t0 + first + e, 0]
            for r in range(max(0, e - CONV_WIDTH + 1), min(CONV_ROWS, e + 1)):
                acc[r] = acc[r] + taps[e - r] * row
        for r in range(CONV_ROWS):
            o_ref[0, t0 + r, :] = acc[r]
        return carry

    lax.fori_loop(0, CONV_TM // CONV_ROWS, body, 0, unroll=4)
    ubuf[pl.ds(0, CONV_HALO)] = ubuf[pl.ds(CONV_TM, CONV_HALO)]


def _glu_conv(x, w_in, b_in, dw, dw_b, riders):
    b_, s, d = x.shape
    nj = d // CONV_TN
    ni = s // CONV_TM
    rider = _Riders(riders, nj * b_ * ni, lambda j, b, i: (j * b_ + b) * ni + i)
    conv, *rounded = pl.pallas_call(
        functools.partial(_glu_conv_kernel, n_riders=len(riders)),
        grid=(nj, b_, ni),
        in_specs=[
            pl.BlockSpec((1, CONV_TM, d), lambda j, b, i: (b, i, 0)),
            pl.BlockSpec((d, CONV_TN), lambda j, b, i: (0, j)),
            pl.BlockSpec((d, CONV_TN), lambda j, b, i: (0, j + nj)),
            pl.BlockSpec((1, CONV_TN), lambda j, b, i: (0, j)),
            pl.BlockSpec((1, CONV_TN), lambda j, b, i: (0, j + nj)),
            pl.BlockSpec((CONV_WIDTH, 1, CONV_TN), lambda j, b, i: (0, 0, j)),
            pl.BlockSpec((1, 1, CONV_TN), lambda j, b, i: (0, 0, j)),
            *rider.specs,
        ],
        out_specs=[pl.BlockSpec((1, CONV_TM, CONV_TN), lambda j, b, i: (b, i, j)), *rider.specs],
        out_shape=[jax.ShapeDtypeStruct((b_, s, d), F32), *rider.out_shapes],
        scratch_shapes=[pltpu.VMEM((CONV_HALO + CONV_TM, 1, CONV_TN), F32)],
        compiler_params=_params("arbitrary", "arbitrary", "arbitrary"),
        name="glu_conv",
    )(x, w_in, w_in, b_in, b_in, dw, dw_b, *rider.operands)
    return conv, rider.restore(rounded)


ROW_TM = 512
ROW_SPLIT = 2


def _proj_ln_kernel(act_ref, x_ref, w_ref, pg_ref, pb_ref, g_ref, b_ref, o_ref, *, conv_prologue):
    for h in range(ROW_SPLIT):
        rows = pl.ds(h * (ROW_TM // ROW_SPLIT), ROW_TM // ROW_SPLIT)
        a = act_ref[rows, :]
        if conv_prologue:
            a = _layer_norm(a, pg_ref[...], pb_ref[...])
            a = a * jax.nn.sigmoid(a)
        mix = _mm(a.astype(BF16), w_ref[...])
        o_ref[rows, :] = _layer_norm(DEEPNORM_ALPHA * x_ref[rows, :] + mix, g_ref[...], b_ref[...])


def _proj_ln(act, x, w, pro_g, pro_b, g, b, *, conv_prologue, name):
    t, d = x.shape
    row = pl.BlockSpec((ROW_TM, d), lambda i: (i, 0))
    vec = pl.BlockSpec((1, d), lambda i: (0, 0))
    return pl.pallas_call(
        functools.partial(_proj_ln_kernel, conv_prologue=conv_prologue),
        grid=(t // ROW_TM,),
        in_specs=[row, row, pl.BlockSpec((d, d), lambda i: (0, 0)), vec, vec, vec, vec],
        out_specs=row,
        out_shape=jax.ShapeDtypeStruct((t, d), F32),
        compiler_params=_params("arbitrary"),
        name=name,
    )(act, x, w, pro_g, pro_b, g, b)


MLP_TM = 1024
MLP_TF = 512
MLP_LAST_ROWS = 256


def _mlp_kernel(x_ref, up_ref, down_ref, g_ref, b_ref, *refs, n_riders):
    rider_in, (o_ref,), rider_out, (xb_ref,) = _split(refs, n_riders, 1, n_riders, 1)
    _round_riders(rider_in, rider_out)
    k = pl.program_id(1)

    def partial_out(rows):
        h = jnp.maximum(_mm(xb_ref[rows, :], up_ref[...]), 0.0)
        return _mm((h * h).astype(BF16), down_ref[...])

    last = pl.num_programs(1) - 1

    @pl.when(k == 0)
    def _():
        xb_ref[...] = x_ref[...].astype(BF16)
        o_ref[...] = partial_out(slice(None))

    @pl.when(jnp.logical_and(k > 0, k < last))
    def _():
        o_ref[...] += partial_out(slice(None))

    @pl.when(k == last)
    def _():
        for r0 in range(0, MLP_TM, MLP_LAST_ROWS):
            rows = pl.ds(r0, MLP_LAST_ROWS)
            y = o_ref[rows, :] + partial_out(rows)
            o_ref[rows, :] = _layer_norm(DEEPNORM_ALPHA * x_ref[rows, :] + y, g_ref[...], b_ref[...])


def _mlp(x, up, down, g, b, *, layer, name, riders=()):
    t, d = x.shape
    ff = up.shape[2]
    nk = ff // MLP_TF
    row = pl.BlockSpec((MLP_TM, d), lambda i, k: (i, 0))
    vec = pl.BlockSpec((1, d), lambda i, k: (0, 0))
    rider = _Riders(riders, (t // MLP_TM) * nk, lambda i, k: i * nk + k)
    out, *rounded = pl.pallas_call(
        functools.partial(_mlp_kernel, n_riders=len(riders)),
        grid=(t // MLP_TM, nk),
        in_specs=[row, pl.BlockSpec((None, d, MLP_TF), lambda i, k: (layer, 0, k)),
                  pl.BlockSpec((None, MLP_TF, d), lambda i, k: (layer, k, 0)), vec, vec, *rider.specs],
        out_specs=[row, *rider.specs],
        out_shape=[jax.ShapeDtypeStruct((t, d), F32), *rider.out_shapes],
        scratch_shapes=[pltpu.VMEM((MLP_TM, d), BF16)],
        compiler_params=_params("arbitrary", "arbitrary"),
        name=name,
    )(x, up, down, g, b, *rider.operands)
    return out, rider.restore(rounded)


def _ple_kernel(x_ref, p_ref, proj_ref, gate_ref, *refs, emit_mxu_inputs):
    if emit_mxu_inputs:
        g_ref, b_ref, o_ref, *mxu_inputs = refs
    else:
        (o_ref,), mxu_inputs = refs, ()
    for h in range(ROW_SPLIT):
        rows = pl.ds(h * (ROW_TM // ROW_SPLIT), ROW_TM // ROW_SPLIT)
        x = x_ref[rows, :]
        emb = _mm(p_ref[rows, :].astype(BF16), proj_ref[...])
        gate = jax.nn.sigmoid(_mm(x.astype(BF16), gate_ref[...]))
        out = x + emb * gate
        o_ref[rows, :] = out
        if mxu_inputs:
            plain_ref, normed_ref = mxu_inputs
            plain_ref[rows, :] = out.astype(BF16)
            normed_ref[rows, :] = _layer_norm(out, g_ref[...], b_ref[...]).astype(BF16)


def _ple(x, p, proj, gate, *, layer, name, kv_ln=None):
    t, d = x.shape
    row = pl.BlockSpec((ROW_TM, d), lambda i: (i, 0))
    vec = pl.BlockSpec((1, d), lambda i: (0, 0))
    emit = kv_ln is not None
    n_out = 3 if emit else 1
    return pl.pallas_call(
        functools.partial(_ple_kernel, emit_mxu_inputs=emit),
        grid=(t // ROW_TM,),
        in_specs=[row, pl.BlockSpec((None, ROW_TM, PLE_DIM), lambda i: (layer, i, 0)),
                  pl.BlockSpec((None, PLE_DIM, d), lambda i: (layer, 0, 0)),
                  pl.BlockSpec((None, d, d), lambda i: (layer, 0, 0))] + ([vec, vec] if emit else []),
        out_specs=[row] * n_out,
        out_shape=[jax.ShapeDtypeStruct((t, d), F32)] + [jax.ShapeDtypeStruct((t, d), BF16)] * (n_out - 1),
        compiler_params=_params("arbitrary"),
        name=name,
    )(x, p, proj, gate, *(kv_ln if emit else ()))


ROPE_TM = 512


def _rope_table_kernel(pos_ref, freq_ref, cos_ref, sin_ref):
    half_rows = ROPE_TM // 2
    low = lax.broadcasted_iota(jnp.int32, (half_rows, HEAD_DIM), 1) < HEAD_DIM // 2
    ang = jnp.where(low, pos_ref[:, 0:1], pos_ref[:, 1:2]) * freq_ref[...]
    c, s = jnp.cos(ang), jnp.sin(ang)
    c_sw, s_sw = pltpu.roll(c, HEAD_DIM // 2, axis=1), pltpu.roll(s, HEAD_DIM // 2, axis=1)
    even, odd = pl.ds(0, half_rows, stride=2), pl.ds(1, half_rows, stride=2)
    cos_ref[even, :] = jnp.where(low, c, c_sw)
    cos_ref[odd, :] = jnp.where(low, c_sw, c)
    sin_ref[even, :] = jnp.where(low, -s, s_sw)
    sin_ref[odd, :] = jnp.where(low, -s_sw, s)


def _rope_tables(positions):
    t = positions.size
    half = HEAD_DIM // 2
    inv_freq = ROPE_THETA ** (-jnp.arange(half, dtype=F32) * (2.0 / HEAD_DIM))
    freq = jnp.concatenate([inv_freq, inv_freq]).reshape(1, HEAD_DIM)
    pos = positions.astype(F32).reshape(t // 2, 2)
    tab = pl.BlockSpec((ROPE_TM, HEAD_DIM), lambda i: (i, 0))
    return pl.pallas_call(
        _rope_table_kernel,
        grid=(t // ROPE_TM,),
        in_specs=[pl.BlockSpec((ROPE_TM // 2, 2), lambda i: (i, 0)), pl.BlockSpec((1, HEAD_DIM), lambda i: (0, 0))],
        out_specs=[tab, tab],
        out_shape=[jax.ShapeDtypeStruct((t, HEAD_DIM), F32)] * 2,
        compiler_params=_params("arbitrary"),
        name="rope_tables",
    )(pos, freq)


def _rotary(v, cos, sin_signed):
    return v * cos + pltpu.roll(v, HEAD_DIM // 2, axis=1) * sin_signed


HEADS_TM = 1024
HEADS_TN = 1024
HEADS_PER_TILE = HEADS_TN // HEAD_DIM
HEADS_CHUNK = 256
HEADS_SUB_M = 256


def _heads_kernel(x_ref, w_ref, cos_ref, sin_ref, *refs, groups, tiles_per_group, n_rotary_tiles):
    n_outs = sum(len(dils) for dils in groups)
    stage_ref, stage2_ref = refs[n_outs:]
    j = pl.program_id(2)

    cos, sin = cos_ref[...], sin_ref[...]
    if n_rotary_tiles is not None:
        rotate = j < n_rotary_tiles
        cos = jnp.where(rotate, cos, 1.0)
        sin = jnp.where(rotate, sin, 0.0)

    at = 0
    for g, dils in enumerate(groups):
        outs = dict(zip(dils, refs[at:at + len(dils)]))
        at += len(dils)
        body = functools.partial(_heads_tile, x_ref, w_ref, cos, sin, outs, stage_ref, stage2_ref)
        if len(groups) == 1:
            body()
        else:
            pl.when(j // tiles_per_group == g)(body)


def _heads_tile(x_ref, w_ref, cos, sin, outs, stage_ref, stage2_ref):
    dilations = tuple(outs)
    assert set(dilations) <= {1, 4, 16}
    quarter = HEADS_SUB_M // 4
    for c in range(HEADS_TN // HEADS_CHUNK):
        for m0 in range(0, HEADS_TM, HEADS_SUB_M):
            res = _mm(x_ref[0, pl.ds(m0, HEADS_SUB_M), :], w_ref[:, c * HEADS_CHUNK:(c + 1) * HEADS_CHUNK])
            for ch in range(HEADS_CHUNK // HEAD_DIM):
                hh = c * (HEADS_CHUNK // HEAD_DIM) + ch
                v = _rotary(res[:, ch * HEAD_DIM:(ch + 1) * HEAD_DIM],
                            cos[m0:m0 + HEADS_SUB_M], sin[m0:m0 + HEADS_SUB_M])
                if 1 in outs:
                    outs[1][0, hh, pl.ds(m0, HEADS_SUB_M), :] = v.astype(BF16)
                if max(dilations) == 1:
                    continue
                stage_ref[hh, pl.ds(m0, HEADS_SUB_M), :] = v
                for r4 in range(4):
                    part = stage_ref[hh, pl.ds(m0 + r4, quarter, stride=4), :]
                    if 4 in outs:
                        outs[4][0, hh, pl.ds(m0 // 4, quarter), r4 * HEAD_DIM:(r4 + 1) * HEAD_DIM] = part.astype(BF16)
                    if 16 not in outs:
                        continue
                    stage2_ref[hh, pl.ds(m0 + r4 * quarter, quarter), :] = part
                    for q4 in range(4):
                        r16 = 4 * q4 + r4
                        sub = stage2_ref[hh, pl.ds(m0 + r4 * quarter + q4, quarter // 4, stride=4), :]
                        outs[16][0, hh, pl.ds(m0 // 16, quarter // 4), r16 * HEAD_DIM:(r16 + 1) * HEAD_DIM] = (
                            sub.astype(BF16))


def _heads_proj(x, w, cos, sin, *, groups, n_rotary_tiles, name):
    b_, s, d = x.shape
    per_b = s // HEADS_TM
    n_tiles = w.shape[1] // HEADS_TN
    tpg = n_tiles // len(groups)
    assert tpg * len(groups) == n_tiles
    tab = pl.BlockSpec((HEADS_TM, HEAD_DIM), lambda bb, i, j: (bb * per_b + i, 0))
    n_heads = tpg * HEADS_PER_TILE
    out_specs, out_shape = [], []
    for g, dils in enumerate(groups):
        for dil in dils:
            out_specs.append(pl.BlockSpec(
                (1, HEADS_PER_TILE, HEADS_TM // dil, dil * HEAD_DIM),
                lambda bb, i, j, g=g: (bb, jnp.clip(j - g * tpg, 0, tpg - 1), i, 0)))
            out_shape.append(jax.ShapeDtypeStruct((b_, n_heads, s // dil, dil * HEAD_DIM), BF16))
    return pl.pallas_call(
        functools.partial(_heads_kernel, groups=groups, tiles_per_group=tpg, n_rotary_tiles=n_rotary_tiles),
        grid=(b_, per_b, n_tiles),
        in_specs=[pl.BlockSpec((1, HEADS_TM, d), lambda bb, i, j: (bb, i, 0)),
                  pl.BlockSpec((d, HEADS_TN), lambda bb, i, j: (0, j)), tab, tab],
        out_specs=out_specs,
        out_shape=out_shape,
        scratch_shapes=[pltpu.VMEM((HEADS_PER_TILE, HEADS_TM, HEAD_DIM), F32)] * 2,
        compiler_params=_params("arbitrary", "arbitrary", "arbitrary"),
        name=name,
    )(x, w, cos, sin)


MERGE_ROWS = 256
LOG2_E = 1.4426950408889634


def _attn_kernel(*refs):
    group_refs = [refs[5 * g:5 * g + 5] for g in range(N_GROUPS)]
    o_ref, og_ref, lse_ref, bias_ref = refs[5 * N_GROUPS:]
    tile = pl.program_id(2)
    scale = HEAD_DIM ** -0.5
    nt_dims = (((1,), (1,)), ((), ()))

    qi = lax.broadcasted_iota(jnp.int32, (ATTN_BLOCK, 2 * ATTN_BLOCK), 0)
    kj = lax.broadcasted_iota(jnp.int32, (ATTN_BLOCK, 2 * ATTN_BLOCK), 1)
    bias = jnp.where(jnp.logical_and(kj >= qi, kj <= qi + ATTN_BLOCK), 0.0, -jnp.inf)
    bias_ref[0] = bias
    bias_ref[1] = jnp.where(jnp.logical_or(tile > 0, kj >= ATTN_BLOCK), bias, -jnp.inf)

    for g, (dil, (q_ref, kc_ref, kp_ref, vc_ref, vp_ref)) in enumerate(zip(DILATIONS, group_refs)):
        n_blocks = ATTN_TILE // (ATTN_BLOCK * dil)
        for r in range(dil):
            lanes = slice(r * HEAD_DIM, (r + 1) * HEAD_DIM)
            k_prev = kp_ref[0, 0, :, lanes]
            v_prev = vp_ref[0, 0, :, lanes]
            for n in range(n_blocks):
                rows = slice(n * ATTN_BLOCK, (n + 1) * ATTN_BLOCK)
                q = q_ref[0, 0, rows, lanes]
                k_cur = kc_ref[0, 0, rows, lanes]
                v_cur = vc_ref[0, 0, rows, lanes]
                keys = jnp.concatenate([k_prev, k_cur], axis=0)
                vals = jnp.concatenate([v_prev, v_cur], axis=0)
                s = lax.dot_general(q, keys, nt_dims, preferred_element_type=F32) + bias_ref[1 if n == 0 else 0]
                m = jnp.max(s, axis=1, keepdims=True)
                p = jnp.exp2((s - m) * (scale * LOG2_E))
                l = jnp.sum(p, axis=1, keepdims=True)
                pv = _mm(p.astype(BF16), vals)
                if dil == 1:
                    dst = pl.ds(n * ATTN_BLOCK, ATTN_BLOCK)
                else:
                    dst = pl.ds(n * ATTN_BLOCK * dil + r, ATTN_BLOCK, stride=dil)
                og_ref[g, dst, :] = pv / l
                lse_ref[g, dst, :] = jnp.broadcast_to(m * scale + jnp.log(l), (ATTN_BLOCK, HEAD_DIM))
                k_prev, v_prev = k_cur, v_cur

    for c in range(ATTN_TILE // MERGE_ROWS):
        rows = slice(c * MERGE_ROWS, (c + 1) * MERGE_ROWS)
        lses = [lse_ref[g, rows, :] for g in range(N_GROUPS)]
        top = functools.reduce(jnp.maximum, lses)
        wts = [jnp.exp(v - top) for v in lses]
        num = sum(w * og_ref[g, rows, :] for g, w in enumerate(wts))
        o_ref[0, rows, :] = (num / sum(wts)).astype(o_ref.dtype)


def _attention(q_views, kv_views):
    hd = HEAD_DIM
    b_ = q_views[0].shape[0]
    s = q_views[0].shape[2]
    n_tiles = s // ATTN_TILE
    operands, in_specs = [], []
    for dil, qv, kvv in zip(DILATIONS, q_views, kv_views):
        rows, width = ATTN_TILE // dil, dil * hd
        prev_per_tile = rows // ATTN_BLOCK

        def cur(head_offset):
            return pl.BlockSpec((1, 1, rows, width), lambda b, h, t, o=head_offset: (b, o + h, t, 0))

        def prev(head_offset):
            return pl.BlockSpec((1, 1, ATTN_BLOCK, width),
                                lambda b, h, t, o=head_offset, n=prev_per_tile: (b, o + h, jnp.maximum(n * t - 1, 0), 0))

        operands += [qv, kvv, kvv, kvv, kvv]
        in_specs += [cur(0), cur(0), prev(0), cur(N_HEADS), prev(N_HEADS)]

    return pl.pallas_call(
        _attn_kernel,
        grid=(b_, N_HEADS, n_tiles),
        in_specs=in_specs,
        out_specs=pl.BlockSpec((1, ATTN_TILE, hd), lambda b, h, t: (b, t, h)),
        out_shape=jax.ShapeDtypeStruct((b_, s, N_HEADS * hd), BF16),
        scratch_shapes=[pltpu.VMEM((N_GROUPS, ATTN_TILE, hd), F32), pltpu.VMEM((N_GROUPS, ATTN_TILE, hd), F32),
                        pltpu.VMEM((2, ATTN_BLOCK, 2 * ATTN_BLOCK), F32)],
        compiler_params=_params("arbitrary", "arbitrary", "arbitrary"),
        name="dilated_attention",
    )(*operands)


def kernel(x, p, positions, conv_w_in, conv_b_in, conv_dw, conv_dw_b, conv_ln_g, conv_ln_b, conv_w_out,
           kv_ln_g, kv_ln_b, w_kv, attn_w_q, attn_w_o, ln1_g, ln1_b, mlp_up, mlp_down, ln2_g, ln2_b,
           ple_proj, ple_gate):
    b_, s, d = x.shape
    t = b_ * s
    assert s % ATTN_TILE == 0 and d == D_MODEL

    def vec(v):
        return v.reshape(1, -1)

    def wb(w):
        return w.astype(BF16)

    cos, sin = _rope_tables(positions)

    pf = p.reshape(DEPTH, t, PLE_DIM)
    conv, (up, down, w_out) = _glu_conv(x, wb(conv_w_in[0]), vec(conv_b_in[0]), conv_dw[0],
                                        conv_dw_b[0].reshape(1, 1, d), [mlp_up, mlp_down, conv_w_out[0]])
    xf = x.reshape(t, d)
    xf = _proj_ln(conv.reshape(t, d), xf, w_out, vec(conv_ln_g[0]), vec(conv_ln_b[0]),
                  vec(ln1_g[0]), vec(ln1_b[0]), conv_prologue=True, name="conv_out_ln1")
    xf, (gate, w_kv_b, w_q, w_o) = _mlp(xf, up, down, vec(ln2_g[0]), vec(ln2_b[0]), layer=0, name="mlp0",
                                        riders=[ple_gate, w_kv, attn_w_q[0], attn_w_o[0]])
    proj = wb(ple_proj)
    xf, x_plain, x_normed = _ple(xf, pf, proj, gate, layer=0, name="ple0", kv_ln=(vec(kv_ln_g), vec(kv_ln_b)))

    kv_views = _heads_proj(x_normed.reshape(b_, s, d), w_kv_b, cos, sin, groups=(DILATIONS,),
                           n_rotary_tiles=d // HEADS_TN, name="kv_proj")
    q_views = _heads_proj(x_plain.reshape(b_, s, d), w_q, cos, sin, groups=tuple((dil,) for dil in DILATIONS),
                          n_rotary_tiles=None, name="q_proj")
    o = _attention(q_views, kv_views)
    xf = _proj_ln(o.reshape(t, d), xf, w_o, vec(ln1_g[1]), vec(ln1_b[1]), vec(ln1_g[1]), vec(ln1_b[1]),
                  conv_prologue=False, name="attn_out_ln1")
    xf, _ = _mlp(xf, up, down, vec(ln2_g[1]), vec(ln2_b[1]), layer=1, name="mlp1")
    xf, = _ple(xf, pf, proj, gate, layer=1, name="ple1")
    return xf.reshape(b_, s, d)
```

```python
import functools

import jax
import jax.numpy as jnp
from jax import lax
from jax.experimental import pallas as pl
from jax.experimental.pallas import tpu as pltpu

D_MODEL = 2048
DEPTH = 2
HEAD_DIM = 128
N_HEADS = D_MODEL // HEAD_DIM
DILATIONS = (1, 4, 16)
N_GROUPS = len(DILATIONS)
ATTN_BLOCK = 128
ATTN_TILE = ATTN_BLOCK * DILATIONS[-1]
CONV_WIDTH = 31
CONV_HALO = 32
D_FF = 4 * D_MODEL
PLE_DIM = 256
ROPE_THETA = 10000.0
LN_EPS = 1e-5
DEEPNORM_ALPHA = (2 * DEPTH) ** 0.25

VMEM_LIMIT_BYTES = 58 * 1024 * 1024
BF16_SUBLANES = 16
BF16 = jnp.bfloat16
F32 = jnp.float32


def _params(*semantics):
    return pltpu.CompilerParams(dimension_semantics=semantics, vmem_limit_bytes=VMEM_LIMIT_BYTES)


def _layer_norm(v, g, b):
    mu = jnp.mean(v, axis=-1, keepdims=True)
    c = v - mu
    var = jnp.mean(c * c, axis=-1, keepdims=True)
    return c * lax.rsqrt(var + LN_EPS) * g + b


def _mm(a, b):
    return jnp.dot(a, b, preferred_element_type=F32)


def _split(refs, *sizes):
    out, at = [], 0
    for n in sizes:
        out.append(refs[at:at + n])
        at += n
    assert at == len(refs)
    return out


class _Riders:
    def __init__(self, arrays, steps, step_of_grid):
        self.shapes = [a.shape for a in arrays]
        self.operands = [a.reshape(-1, a.shape[-1]) for a in arrays]
        self.specs, self.out_shapes = [], []
        for a in self.operands:
            rows, width = a.shape[0] // steps, a.shape[1]
            assert rows * steps == a.shape[0] and rows % BF16_SUBLANES == 0
            self.specs.append(pl.BlockSpec((rows, width), lambda *g: (step_of_grid(*g), 0)))
            self.out_shapes.append(jax.ShapeDtypeStruct(a.shape, BF16))

    def restore(self, rounded):
        return [r.reshape(s) for r, s in zip(rounded, self.shapes)]


def _round_riders(in_refs, out_refs):
    for src, dst in zip(in_refs, out_refs):
        dst[...] = src[...].astype(BF16)


CONV_TM = 256
CONV_TN = 1024
CONV_ROWS = 16


def _glu_conv_kernel(x_ref, wa_ref, wg_ref, ba_ref, bg_ref, dw_ref, dwb_ref, *refs, n_riders):
    rider_in, (o_ref,), rider_out, (ubuf,) = _split(refs, n_riders, 1, n_riders, 1)
    _round_riders(rider_in, rider_out)
    i = pl.program_id(2)

    @pl.when(i == 0)
    def _():
        ubuf[pl.ds(0, CONV_HALO)] = jnp.zeros((CONV_HALO, 1, CONV_TN), F32)

    xb = x_ref[0].astype(BF16)
    a = _mm(xb, wa_ref[...]) + ba_ref[...]
    g = _mm(xb, wg_ref[...]) + bg_ref[...]
    ubuf[pl.ds(CONV_HALO, CONV_TM), 0, :] = a * jax.nn.sigmoid(g)

    taps = [dw_ref[k, 0] for k in range(CONV_WIDTH)]
    bias = dwb_ref[0, 0]
    first = CONV_HALO - (CONV_WIDTH - 1)

    def body(c, carry):
        t0 = c * CONV_ROWS
        acc = [bias] * CONV_ROWS
        for e in range(CONV_ROWS + CONV_WIDTH - 1):
            row = ubuf[t0 + first + e, 0]
            for r in range(max(0, e - CONV_WIDTH + 1), min(CONV_ROWS, e + 1)):
                acc[r] = acc[r] + taps[e - r] * row
        for r in range(CONV_ROWS):
            o_ref[0, t0 + r, :] = acc[r]
        return carry

    lax.fori_loop(0, CONV_TM // CONV_ROWS, body, 0, unroll=4)
    ubuf[pl.ds(0, CONV_HALO)] = ubuf[pl.ds(CONV_TM, CONV_HALO)]


def _glu_conv(x, w_in, b_in, dw, dw_b, riders):
    b_, s, d = x.shape
    nj = d // CONV_TN
    ni = s // CONV_TM
    rider = _Riders(riders, nj * b_ * ni, lambda j, b, i: (j * b_ + b) * ni + i)
    conv, *rounded = pl.pallas_call(
        functools.partial(_glu_conv_kernel, n_riders=len(riders)),
        grid=(nj, b_, ni),
        in_specs=[
            pl.BlockSpec((1, CONV_TM, d), lambda j, b, i: (b, i, 0)),
            pl.BlockSpec((d, CONV_TN), lambda j, b, i: (0, j)),
            pl.BlockSpec((d, CONV_TN), lambda j, b, i: (0, j + nj)),
            pl.BlockSpec((1, CONV_TN), lambda j, b, i: (0, j)),
            pl.BlockSpec((1, CONV_TN), lambda j, b, i: (0, j + nj)),
            pl.BlockSpec((CONV_WIDTH, 1, CONV_TN), lambda j, b, i: (0, 0, j)),
            pl.BlockSpec((1, 1, CONV_TN), lambda j, b, i: (0, 0, j)),
            *rider.specs,
        ],
        out_specs=[pl.BlockSpec((1, CONV_TM, CONV_TN), lambda j, b, i: (b, i, j)), *rider.specs],
        out_shape=[jax.ShapeDtypeStruct((b_, s, d), F32), *rider.out_shapes],
        scratch_shapes=[pltpu.VMEM((CONV_HALO + CONV_TM, 1, CONV_TN), F32)],
        compiler_params=_params("arbitrary", "arbitrary", "arbitrary"),
        name="glu_conv",
    )(x, w_in, w_in, b_in, b_in, dw, dw_b, *rider.operands)
    return conv, rider.restore(rounded)


ROW_TM = 512
ROW_SPLIT = 2


def _proj_ln_kernel(act_ref, x_ref, w_ref, pg_ref, pb_ref, g_ref, b_ref, o_ref, *, conv_prologue):
    for h in range(ROW_SPLIT):
        rows = pl.ds(h * (ROW_TM // ROW_SPLIT), ROW_TM // ROW_SPLIT)
        a = act_ref[rows, :]
        if conv_prologue:
            a = _layer_norm(a, pg_ref[...], pb_ref[...])
            a = a * jax.nn.sigmoid(a)
        mix = _mm(a.astype(BF16), w_ref[...])
        o_ref[rows, :] = _layer_norm(DEEPNORM_ALPHA * x_ref[rows, :] + mix, g_ref[...], b_ref[...])


def _proj_ln(act, x, w, pro_g, pro_b, g, b, *, conv_prologue, name):
    t, d = x.shape
    row = pl.BlockSpec((ROW_TM, d), lambda i: (i, 0))
    vec = pl.BlockSpec((1, d), lambda i: (0, 0))
    return pl.pallas_call(
        functools.partial(_proj_ln_kernel, conv_prologue=conv_prologue),
        grid=(t // ROW_TM,),
        in_specs=[row, row, pl.BlockSpec((d, d), lambda i: (0, 0)), vec, vec, vec, vec],
        out_specs=row,
        out_shape=jax.ShapeDtypeStruct((t, d), F32),
        compiler_params=_params("arbitrary"),
        name=name,
    )(act, x, w, pro_g, pro_b, g, b)


MLP_TM = 1024
MLP_TF = 512
MLP_LAST_ROWS = 256


def _mlp_kernel(x_ref, up_ref, down_ref, g_ref, b_ref, *refs, n_riders):
    rider_in, (o_ref,), rider_out, (xb_ref,) = _split(refs, n_riders, 1, n_riders, 1)
    _round_riders(rider_in, rider_out)
    k = pl.program_id(1)

    def partial_out(rows):
        out = None
        for c0 in range(0, MLP_TF, MLP_TF // 2):
            h = jnp.maximum(_mm(xb_ref[rows, :], up_ref[:, c0:c0 + MLP_TF // 2]), 0.0)
            term = _mm((h * h).astype(BF16), down_ref[c0:c0 + MLP_TF // 2, :])
            out = term if out is None else out + term
        return out

    last = pl.num_programs(1) - 1

    @pl.when(k == 0)
    def _():
        xb_ref[...] = x_ref[...].astype(BF16)
        o_ref[...] = partial_out(slice(None))

    @pl.when(jnp.logical_and(k > 0, k < last))
    def _():
        o_ref[...] += partial_out(slice(None))

    @pl.when(k == last)
    def _():
        for r0 in range(0, MLP_TM, MLP_LAST_ROWS):
            rows = pl.ds(r0, MLP_LAST_ROWS)
            y = o_ref[rows, :] + partial_out(rows)
            o_ref[rows, :] = _layer_norm(DEEPNORM_ALPHA * x_ref[rows, :] + y, g_ref[...], b_ref[...])


def _mlp(x, up, down, g, b, *, layer, name, riders=()):
    t, d = x.shape
    ff = up.shape[2]
    nk = ff // MLP_TF
    row = pl.BlockSpec((MLP_TM, d), lambda i, k: (i, 0))
    vec = pl.BlockSpec((1, d), lambda i, k: (0, 0))
    rider = _Riders(riders, (t // MLP_TM) * nk, lambda i, k: i * nk + k)
    out, *rounded = pl.pallas_call(
        functools.partial(_mlp_kernel, n_riders=len(riders)),
        grid=(t // MLP_TM, nk),
        in_specs=[row, pl.BlockSpec((None, d, MLP_TF), lambda i, k: (layer, 0, k)),
                  pl.BlockSpec((None, MLP_TF, d), lambda i, k: (layer, k, 0)), vec, vec, *rider.specs],
        out_specs=[row, *rider.specs],
        out_shape=[jax.ShapeDtypeStruct((t, d), F32), *rider.out_shapes],
        scratch_shapes=[pltpu.VMEM((MLP_TM, d), BF16)],
        compiler_params=_params("arbitrary", "arbitrary"),
        name=name,
    )(x, up, down, g, b, *rider.operands)
    return out, rider.restore(rounded)


def _ple_kernel(x_ref, p_ref, proj_ref, gate_ref, *refs, emit_mxu_inputs):
    if emit_mxu_inputs:
        g_ref, b_ref, o_ref, *mxu_inputs = refs
    else:
        (o_ref,), mxu_inputs = refs, ()
    for h in range(ROW_SPLIT):
        rows = pl.ds(h * (ROW_TM // ROW_SPLIT), ROW_TM // ROW_SPLIT)
        x = x_ref[rows, :]
        emb = _mm(p_ref[rows, :].astype(BF16), proj_ref[...])
        gate = jax.nn.sigmoid(_mm(x.astype(BF16), gate_ref[...]))
        out = x + emb * gate
        o_ref[rows, :] = out
        if mxu_inputs:
            plain_ref, normed_ref = mxu_inputs
            plain_ref[rows, :] = out.astype(BF16)
            normed_ref[rows, :] = _layer_norm(out, g_ref[...], b_ref[...]).astype(BF16)


def _ple(x, p, proj, gate, *, layer, name, kv_ln=None):
    t, d = x.shape
    row = pl.BlockSpec((ROW_TM, d), lambda i: (i, 0))
    vec = pl.BlockSpec((1, d), lambda i: (0, 0))
    emit = kv_ln is not None
    n_out = 3 if emit else 1
    return pl.pallas_call(
        functools.partial(_ple_kernel, emit_mxu_inputs=emit),
        grid=(t // ROW_TM,),
        in_specs=[row, pl.BlockSpec((None, ROW_TM, PLE_DIM), lambda i: (layer, i, 0)),
                  pl.BlockSpec((None, PLE_DIM, d), lambda i: (layer, 0, 0)),
                  pl.BlockSpec((None, d, d), lambda i: (layer, 0, 0))] + ([vec, vec] if emit else []),
        out_specs=[row] * n_out,
        out_shape=[jax.ShapeDtypeStruct((t, d), F32)] + [jax.ShapeDtypeStruct((t, d), BF16)] * (n_out - 1),
        compiler_params=_params("arbitrary"),
        name=name,
    )(x, p, proj, gate, *(kv_ln if emit else ()))


ROPE_TM = 512


def _rope_table_kernel(pos_ref, freq_ref, cos_ref, sin_ref):
    half_rows = ROPE_TM // 2
    low = lax.broadcasted_iota(jnp.int32, (half_rows, HEAD_DIM), 1) < HEAD_DIM // 2
    ang = jnp.where(low, pos_ref[:, 0:1], pos_ref[:, 1:2]) * freq_ref[...]
    c, s = jnp.cos(ang), jnp.sin(ang)
    c_sw, s_sw = pltpu.roll(c, HEAD_DIM // 2, axis=1), pltpu.roll(s, HEAD_DIM // 2, axis=1)
    even, odd = pl.ds(0, half_rows, stride=2), pl.ds(1, half_rows, stride=2)
    cos_ref[even, :] = jnp.where(low, c, c_sw)
    cos_ref[odd, :] = jnp.where(low, c_sw, c)
    sin_ref[even, :] = jnp.where(low, -s, s_sw)
    sin_ref[odd, :] = jnp.where(low, -s_sw, s)


def _rope_tables(positions):
    t = positions.size
    half = HEAD_DIM // 2
    inv_freq = ROPE_THETA ** (-jnp.arange(half, dtype=F32) * (2.0 / HEAD_DIM))
    freq = jnp.concatenate([inv_freq, inv_freq]).reshape(1, HEAD_DIM)
    pos = positions.astype(F32).reshape(t // 2, 2)
    tab = pl.BlockSpec((ROPE_TM, HEAD_DIM), lambda i: (i, 0))
    return pl.pallas_call(
        _rope_table_kernel,
        grid=(t // ROPE_TM,),
        in_specs=[pl.BlockSpec((ROPE_TM // 2, 2), lambda i: (i, 0)), pl.BlockSpec((1, HEAD_DIM), lambda i: (0, 0))],
        out_specs=[tab, tab],
        out_shape=[jax.ShapeDtypeStruct((t, HEAD_DIM), F32)] * 2,
        compiler_params=_params("arbitrary"),
        name="rope_tables",
    )(pos, freq)


def _rotary(v, cos, sin_signed):
    return v * cos + pltpu.roll(v, HEAD_DIM // 2, axis=1) * sin_signed


HEADS_TM = 1024
HEADS_TN = 1024
HEADS_PER_TILE = HEADS_TN // HEAD_DIM
HEADS_CHUNK = 256
HEADS_SUB_M = 256


def _heads_kernel(x_ref, w_ref, cos_ref, sin_ref, *refs, dilations, n_rotary_tiles):
    assert set(dilations) <= {1, 4, 16}
    outs = dict(zip(dilations, refs[:len(dilations)]))
    stage_ref, stage2_ref = refs[len(dilations):]
    quarter = HEADS_SUB_M // 4
    j = pl.program_id(2)

    cos, sin = cos_ref[...], sin_ref[...]
    if n_rotary_tiles is not None:
        rotate = j < n_rotary_tiles
        cos = jnp.where(rotate, cos, 1.0)
        sin = jnp.where(rotate, sin, 0.0)

    for c in range(HEADS_TN // HEADS_CHUNK):
        for m0 in range(0, HEADS_TM, HEADS_SUB_M):
            res = _mm(x_ref[0, pl.ds(m0, HEADS_SUB_M), :], w_ref[:, c * HEADS_CHUNK:(c + 1) * HEADS_CHUNK])
            for ch in range(HEADS_CHUNK // HEAD_DIM):
                hh = c * (HEADS_CHUNK // HEAD_DIM) + ch
                v = _rotary(res[:, ch * HEAD_DIM:(ch + 1) * HEAD_DIM],
                            cos[m0:m0 + HEADS_SUB_M], sin[m0:m0 + HEADS_SUB_M])
                if 1 in outs:
                    outs[1][0, hh, pl.ds(m0, HEADS_SUB_M), :] = v.astype(BF16)
                if max(dilations) == 1:
                    continue
                stage_ref[hh, pl.ds(m0, HEADS_SUB_M), :] = v
                for r4 in range(4):
                    part = stage_ref[hh, pl.ds(m0 + r4, quarter, stride=4), :]
                    if 4 in outs:
                        outs[4][0, hh, pl.ds(m0 // 4, quarter), r4 * HEAD_DIM:(r4 + 1) * HEAD_DIM] = part.astype(BF16)
                    if 16 not in outs:
                        continue
                    stage2_ref[hh, pl.ds(m0 + r4 * quarter, quarter), :] = part
                    for q4 in range(4):
                        r16 = 4 * q4 + r4
                        sub = stage2_ref[hh, pl.ds(m0 + r4 * quarter + q4, quarter // 4, stride=4), :]
                        outs[16][0, hh, pl.ds(m0 // 16, quarter // 4), r16 * HEAD_DIM:(r16 + 1) * HEAD_DIM] = (
                            sub.astype(BF16))


def _heads_proj(x, w, cos, sin, *, n_out, col_tile_offset, dilations, n_rotary_tiles, name):
    b_, s, d = x.shape
    per_b = s // HEADS_TM
    tab = pl.BlockSpec((HEADS_TM, HEAD_DIM), lambda bb, i, j: (bb * per_b + i, 0))
    n_heads = n_out // HEAD_DIM
    return pl.pallas_call(
        functools.partial(_heads_kernel, dilations=dilations, n_rotary_tiles=n_rotary_tiles),
        grid=(b_, per_b, n_out // HEADS_TN),
        in_specs=[pl.BlockSpec((1, HEADS_TM, d), lambda bb, i, j: (bb, i, 0)),
                  pl.BlockSpec((d, HEADS_TN), lambda bb, i, j: (0, col_tile_offset + j)), tab, tab],
        out_specs=[pl.BlockSpec((1, HEADS_PER_TILE, HEADS_TM // dil, dil * HEAD_DIM), lambda bb, i, j: (bb, j, i, 0))
                   for dil in dilations],
        out_shape=[jax.ShapeDtypeStruct((b_, n_heads, s // dil, dil * HEAD_DIM), BF16) for dil in dilations],
        scratch_shapes=[pltpu.VMEM((HEADS_PER_TILE, HEADS_TM, HEAD_DIM), F32)] * 2,
        compiler_params=_params("arbitrary", "arbitrary", "arbitrary"),
        name=name,
    )(x, w, cos, sin)


MERGE_ROWS = 256
LOG2_E = 1.4426950408889634


def _attn_kernel(*refs):
    group_refs = [refs[5 * g:5 * g + 5] for g in range(N_GROUPS)]
    o_ref, og_ref, lse_ref, bias_ref = refs[5 * N_GROUPS:]
    tile = pl.program_id(2)
    scale = HEAD_DIM ** -0.5
    nt_dims = (((1,), (1,)), ((), ()))

    qi = lax.broadcasted_iota(jnp.int32, (ATTN_BLOCK, 2 * ATTN_BLOCK), 0)
    kj = lax.broadcasted_iota(jnp.int32, (ATTN_BLOCK, 2 * ATTN_BLOCK), 1)
    bias = jnp.where(jnp.logical_and(kj >= qi, kj <= qi + ATTN_BLOCK), 0.0, -jnp.inf)
    bias_ref[0] = bias
    bias_ref[1] = jnp.where(jnp.logical_or(tile > 0, kj >= ATTN_BLOCK), bias, -jnp.inf)

    for g, (dil, (q_ref, kc_ref, kp_ref, vc_ref, vp_ref)) in enumerate(zip(DILATIONS, group_refs)):
        n_blocks = ATTN_TILE // (ATTN_BLOCK * dil)
        for r in range(dil):
            lanes = slice(r * HEAD_DIM, (r + 1) * HEAD_DIM)
            k_prev = kp_ref[0, 0, :, lanes]
            v_prev = vp_ref[0, 0, :, lanes]
            for n in range(n_blocks):
                rows = slice(n * ATTN_BLOCK, (n + 1) * ATTN_BLOCK)
                q = q_ref[0, 0, rows, lanes]
                k_cur = kc_ref[0, 0, rows, lanes]
                v_cur = vc_ref[0, 0, rows, lanes]
                keys = jnp.concatenate([k_prev, k_cur], axis=0)
                vals = jnp.concatenate([v_prev, v_cur], axis=0)
                s = lax.dot_general(q, keys, nt_dims, preferred_element_type=F32) + bias_ref[1 if n == 0 else 0]
                m = jnp.max(s, axis=1, keepdims=True)
                p = jnp.exp2((s - m) * (scale * LOG2_E))
                l = jnp.sum(p, axis=1, keepdims=True)
                pv = _mm(p.astype(BF16), vals)
                if dil == 1:
                    dst = pl.ds(n * ATTN_BLOCK, ATTN_BLOCK)
                else:
                    dst = pl.ds(n * ATTN_BLOCK * dil + r, ATTN_BLOCK, stride=dil)
                og_ref[g, dst, :] = pv / l
                lse_ref[g, dst, :] = jnp.broadcast_to(m * scale + jnp.log(l), (ATTN_BLOCK, HEAD_DIM))
                k_prev, v_prev = k_cur, v_cur

    for c in range(ATTN_TILE // MERGE_ROWS):
        rows = slice(c * MERGE_ROWS, (c + 1) * MERGE_ROWS)
        lses = [lse_ref[g, rows, :] for g in range(N_GROUPS)]
        top = functools.reduce(jnp.maximum, lses)
        wts = [jnp.exp(v - top) for v in lses]
        num = sum(w * og_ref[g, rows, :] for g, w in enumerate(wts))
        o_ref[0, rows, :] = (num / sum(wts)).astype(o_ref.dtype)


def _attention(q_views, kv_views):
    hd = HEAD_DIM
    b_ = q_views[0].shape[0]
    s = q_views[0].shape[2]
    n_tiles = s // ATTN_TILE
    operands, in_specs = [], []
    for dil, qv, kvv in zip(DILATIONS, q_views, kv_views):
        rows, width = ATTN_TILE // dil, dil * hd
        prev_per_tile = rows // ATTN_BLOCK

        def cur(head_offset):
            return pl.BlockSpec((1, 1, rows, width), lambda b, h, t, o=head_offset: (b, o + h, t, 0))

        def prev(head_offset):
            return pl.BlockSpec((1, 1, ATTN_BLOCK, width),
                                lambda b, h, t, o=head_offset, n=prev_per_tile: (b, o + h, jnp.maximum(n * t - 1, 0), 0))

        operands += [qv, kvv, kvv, kvv, kvv]
        in_specs += [cur(0), cur(0), prev(0), cur(N_HEADS), prev(N_HEADS)]

    return pl.pallas_call(
        _attn_kernel,
        grid=(b_, N_HEADS, n_tiles),
        in_specs=in_specs,
        out_specs=pl.BlockSpec((1, ATTN_TILE, hd), lambda b, h, t: (b, t, h)),
        out_shape=jax.ShapeDtypeStruct((b_, s, N_HEADS * hd), BF16),
        scratch_shapes=[pltpu.VMEM((N_GROUPS, ATTN_TILE, hd), F32), pltpu.VMEM((N_GROUPS, ATTN_TILE, hd), F32),
                        pltpu.VMEM((2, ATTN_BLOCK, 2 * ATTN_BLOCK), F32)],
        compiler_params=_params("arbitrary", "arbitrary", "arbitrary"),
        name="dilated_attention",
    )(*operands)


def kernel(x, p, positions, conv_w_in, conv_b_in, conv_dw, conv_dw_b, conv_ln_g, conv_ln_b, conv_w_out,
           kv_ln_g, kv_ln_b, w_kv, attn_w_q, attn_w_o, ln1_g, ln1_b, mlp_up, mlp_down, ln2_g, ln2_b,
           ple_proj, ple_gate):
    b_, s, d = x.shape
    t = b_ * s
    assert s % ATTN_TILE == 0 and d == D_MODEL

    def vec(v):
        return v.reshape(1, -1)

    def wb(w):
        return w.astype(BF16)

    cos, sin = _rope_tables(positions)

    pf = p.reshape(DEPTH, t, PLE_DIM)
    conv, (up, down, w_out) = _glu_conv(x, wb(conv_w_in[0]), vec(conv_b_in[0]), conv_dw[0],
                                        conv_dw_b[0].reshape(1, 1, d), [mlp_up, mlp_down, conv_w_out[0]])
    xf = x.reshape(t, d)
    xf = _proj_ln(conv.reshape(t, d), xf, w_out, vec(conv_ln_g[0]), vec(conv_ln_b[0]),
                  vec(ln1_g[0]), vec(ln1_b[0]), conv_prologue=True, name="conv_out_ln1")
    xf, (gate, w_kv_b, w_q, w_o) = _mlp(xf, up, down, vec(ln2_g[0]), vec(ln2_b[0]), layer=0, name="mlp0",
                                        riders=[ple_gate, w_kv, attn_w_q[0], attn_w_o[0]])
    proj = wb(ple_proj)
    xf, x_plain, x_normed = _ple(xf, pf, proj, gate, layer=0, name="ple0", kv_ln=(vec(kv_ln_g), vec(kv_ln_b)))

    kv_views = _heads_proj(x_normed.reshape(b_, s, d), w_kv_b, cos, sin, n_out=2 * d, col_tile_offset=0,
                           dilations=DILATIONS, n_rotary_tiles=d // HEADS_TN, name="kv_proj")
    q_views = [
        _heads_proj(x_plain.reshape(b_, s, d), w_q, cos, sin, n_out=d, col_tile_offset=g * (d // HEADS_TN),
                    dilations=(dil,), n_rotary_tiles=None, name=f"q_proj_d{dil}")[0]
        for g, dil in enumerate(DILATIONS)]
    o = _attention(q_views, kv_views)
    xf = _proj_ln(o.reshape(t, d), xf, w_o, vec(ln1_g[1]), vec(ln1_b[1]), vec(ln1_g[1]), vec(ln1_b[1]),
                  conv_prologue=False, name="attn_out_ln1")
    xf, _ = _mlp(xf, up, down, vec(ln2_g[1]), vec(ln2_b[1]), layer=1, name="mlp1")
    xf, = _ple(xf, pf, proj, gate, layer=1, name="ple1")
    return xf.reshape(b_, s, d)
```
